```python
import math
import jax
import jax.numpy as jnp
from jax import lax
import numpy as np

D_MODEL = 2048
BATCH = 4
SEQ = 2048
DEPTH = 4
DEC_BATCH = 8
DEC_SEQ = 1
PAST_LEN = 16384
PAGE_SIZE = 128

HEAD_DIM = 128
MIX_WIDTH = D_MODEL
ATTN_WIDTH = MIX_WIDTH // 2
RET_WIDTH = MIX_WIDTH - ATTN_WIDTH
N_ATTN_HEADS = ATTN_WIDTH // HEAD_DIM
N_RET_HEADS = RET_WIDTH // HEAD_DIM
IN_WIDTH = 3 * ATTN_WIDTH + 4 * RET_WIDTH
MOBA_BLOCK = 256
MOBA_TOPK = 3
MOBA_Q_CHUNK = 32
RET_CHUNK = 128
ROPE_BASE = 10000.0
NUM_BUCKETS = 32
MAX_DISTANCE = 128
D_FF = 11 * D_MODEL // 4
CONV_W = 3
PLE_DIM = 256
LN_EPS = 1e-5
DEEPNORM_ALPHA = (2 * DEPTH) ** 0.25
DEEPNORM_BETA = (8 * DEPTH) ** -0.25

kernel_name = "hymba_moba_retnet_convffn_decode_step"


def layer_norm(x, g, b):
    xf = x.astype(jnp.float32)
    mu = jnp.mean(xf, -1, keepdims=True)
    var = jnp.mean(jnp.square(xf - mu), -1, keepdims=True)
    return ((xf - mu) * lax.rsqrt(var + LN_EPS) * g + b).astype(x.dtype)


def head_norm(o):
    mu = jnp.mean(o, -1, keepdims=True)
    var = jnp.mean(jnp.square(o - mu), -1, keepdims=True)
    return (o - mu) * lax.rsqrt(var + LN_EPS)


def t5_bucket(rel):
    n = jnp.maximum(rel, 0)
    max_exact = NUM_BUCKETS // 2
    nf = jnp.maximum(n, max_exact).astype(jnp.float32)
    large = max_exact + (jnp.log(nf / max_exact) / math.log(MAX_DISTANCE / max_exact)
                         * (NUM_BUCKETS - max_exact)).astype(jnp.int32)
    return jnp.where(n < max_exact, n, jnp.minimum(large, NUM_BUCKETS - 1))


def rotary(x, pos):
    half = HEAD_DIM // 2
    inv = ROPE_BASE ** (-jnp.arange(half, dtype=jnp.float32) / half)
    ang = pos[:, None] * inv[None, :]
    cos = jnp.cos(ang)[None, :, None, :]
    sin = jnp.sin(ang)[None, :, None, :]
    xf = x.astype(jnp.float32)
    x1, x2 = xf[..., :half], xf[..., half:]
    return jnp.concatenate([x1 * cos - x2 * sin, x2 * cos + x1 * sin], -1)


def moba_attention(q, k, v, q_start, rel_bias):
    B, Nq, H, hd = q.shape
    L = k.shape[1]
    nb = -(-L // MOBA_BLOCK)
    pad = nb * MOBA_BLOCK - L
    kb = jnp.pad(k, ((0, 0), (0, pad), (0, 0), (0, 0))).reshape(B, nb, MOBA_BLOCK, H, hd).transpose(0, 3, 1, 2, 4)
    vb = jnp.pad(v, ((0, 0), (0, pad), (0, 0), (0, 0))).reshape(B, nb, MOBA_BLOCK, H, hd).transpose(0, 3, 1, 2, 4)
    k_mean = jnp.mean(kb.astype(jnp.float32), axis=3)
    qc = min(MOBA_Q_CHUNK, Nq)
    n_chunks = -(-Nq // qc)
    qpad = n_chunks * qc - Nq
    qh = jnp.pad(q, ((0, 0), (0, qpad), (0, 0), (0, 0))).transpose(0, 2, 1, 3)
    qh = qh.reshape(B, H, n_chunks, qc, hd).transpose(2, 0, 1, 3, 4)
    qpos = jnp.minimum(q_start + jnp.arange(n_chunks * qc, dtype=jnp.int32), q_start + Nq - 1).reshape(n_chunks, qc)
    n_sel = min(MOBA_TOPK, nb)
    bi = jnp.arange(B)[:, None, None, None]
    hi = jnp.arange(H)[None, :, None, None]
    bias_table = rel_bias.T.astype(jnp.float32)
    scale = HEAD_DIM ** -0.5

    def one_chunk(args):
        qb, pos = args
        cur = pos // MOBA_BLOCK
        gate = jnp.einsum('bhqd,bhnd->bhqn', qb.astype(jnp.float32), k_mean)
        eligible = jnp.arange(nb)[None, :] < cur[:, None]
        gate = jnp.where(eligible, gate, -jnp.inf)
        _, top = lax.top_k(gate, n_sel)
        own = jnp.broadcast_to(cur[None, None, :, None], (B, H, qc, 1)).astype(top.dtype)
        idx = jnp.concatenate([top, own], -1)
        sel_ok = jnp.arange(n_sel)[None, :] < cur[:, None]
        ok = jnp.concatenate([sel_ok, jnp.ones((qc, 1), bool)], -1)
        kg = kb[bi, hi, idx]
        vg = vb[bi, hi, idx]
        kpos = idx[..., None] * MOBA_BLOCK + jnp.arange(MOBA_BLOCK, dtype=idx.dtype)
        rel = pos[:, None, None] - kpos
        bias = bias_table[hi[..., None], t5_bucket(rel)]
        mask = ok[None, None, :, :, None] & (rel >= 0)
        logits = jnp.einsum('bhqd,bhqskd->bhqsk', qb, kg, preferred_element_type=jnp.float32) * scale + bias
        logits = jnp.where(mask, logits, -jnp.inf)
        w = jax.nn.softmax(logits.reshape(B, H, qc, -1), axis=-1).reshape(mask.shape).astype(vg.dtype)
        return jnp.einsum('bhqsk,bhqskd->bhqd', w, vg)

    out = lax.map(one_chunk, (qh, qpos))
    out = out.transpose(1, 0, 3, 2, 4).reshape(B, n_chunks * qc, H, hd)
    return out[:, :Nq]


def retention(q, k, v, s0):
    B, N, H, d = q.shape
    C = math.gcd(N, RET_CHUNK)
    nc = N // C

    def chunks(a):
        return a.reshape(B, nc, C, H, d).transpose(1, 0, 3, 2, 4)

    lg = jnp.log(1.0 - 2.0 ** (-5.0 - jnp.arange(H, dtype=jnp.float32)))
    i = jnp.arange(C, dtype=jnp.float32)
    diff = i[:, None] - i[None, :]
    dmask = jnp.where(diff >= 0, jnp.exp(lg[:, None, None] * jnp.maximum(diff, 0.0)), 0.0)
    cross = jnp.exp(lg[:, None] * (i + 1.0))[..., None]
    upd = jnp.exp(lg[:, None] * (C - 1.0 - i))[..., None]
    cdec = jnp.exp(lg * C)[:, None, None]

    def step(s, qkv):
        qc, kc, vc = qkv
        sc = jnp.einsum('bhid,bhjd->bhij', qc, kc) * dmask
        o = jnp.einsum('bhij,bhjd->bhid', sc, vc) + jnp.einsum('bhid,bhde->bhie', qc, s) * cross
        s = s * cdec + jnp.einsum('bhjd,bhje->bhde', kc * upd, vc)
        return s, o

    s, o = lax.scan(step, s0, (chunks(q), chunks(k), chunks(v)))
    o = o.transpose(1, 0, 3, 2, 4).reshape(B, N, H, d)
    return o, s


def decoder_layer(x, p, past_k, past_v, s0, conv_buf, rel_bias, w_in, w_out, ln1_g, ln1_b,
                  w_gate, w_up, conv_w, conv_b, w_down, ln2_g, ln2_b, w_ple, w_ple_gate):
    B, N, _ = x.shape
    q_start = past_k.shape[1]
    h = x @ w_in
    A, R = ATTN_WIDTH, RET_WIDTH
    qa, ka, va, qr, kr, vr, gr = jnp.split(h, [A, 2 * A, 3 * A, 3 * A + R, 3 * A + 2 * R, 3 * A + 3 * R], axis=-1)
    qa = qa.reshape(B, N, N_ATTN_HEADS, HEAD_DIM)
    ka = ka.reshape(B, N, N_ATTN_HEADS, HEAD_DIM).astype(past_k.dtype)
    va = va.reshape(B, N, N_ATTN_HEADS, HEAD_DIM).astype(past_v.dtype)
    k_all = jnp.concatenate([past_k, ka], axis=1)
    v_all = jnp.concatenate([past_v, va], axis=1)
    attn = moba_attention(qa, k_all, v_all, q_start, rel_bias).reshape(B, N, A).astype(x.dtype)
    pos = jnp.arange(N, dtype=jnp.float32) + float(q_start)
    qr = rotary(qr.reshape(B, N, N_RET_HEADS, HEAD_DIM), pos)
    kr = rotary(kr.reshape(B, N, N_RET_HEADS, HEAD_DIM), pos) * (HEAD_DIM ** -0.5)
    vr = vr.reshape(B, N, N_RET_HEADS, HEAD_DIM).astype(jnp.float32)
    ret, s_new = retention(qr, kr, vr, s0.astype(jnp.float32))
    ret = (jax.nn.silu(gr.astype(jnp.float32)) * head_norm(ret).reshape(B, N, R)).astype(x.dtype)
    mix = jnp.concatenate([attn, ret], axis=-1) @ w_out
    x = layer_norm(DEEPNORM_ALPHA * x + mix, ln1_g, ln1_b)
    a = x @ w_gate
    a_ext = jnp.concatenate([conv_buf.astype(a.dtype), a], axis=1)
    a_conv = conv_b + conv_w[0] * a_ext[:, 0:N]
    for j in range(1, CONV_W):
        a_conv = a_conv + conv_w[j] * a_ext[:, j:j + N]
    conv_new = a_ext[:, -(CONV_W - 1):]
    f = (jax.nn.silu(a_conv) * (x @ w_up)) @ w_down
    x = layer_norm(DEEPNORM_ALPHA * x + f, ln2_g, ln2_b)
    x = x + jax.nn.sigmoid(x @ w_ple_gate) * (p @ w_ple)
    return x, ka, va, s_new, conv_new


def setup_inputs(seed: int = 0) -> dict:
    key = jax.random.key(seed)
    ks = jax.random.split(key, 24)
    n_pages = PAST_LEN // PAGE_SIZE
    n_used = DEC_BATCH * n_pages
    n_pool = (5 * n_used) // 4
    perm = jax.random.permutation(ks[0], n_pool)
    page_table = perm[:n_used].reshape(DEC_BATCH, n_pages).astype(jnp.int32)

    def nrm(k, shape, s):
        return jax.random.normal(k, shape, jnp.float32) * s

    A, R = ATTN_WIDTH, RET_WIDTH
    col_scale = jnp.ones((IN_WIDTH,), jnp.float32)
    col_scale = col_scale.at[2 * A:3 * A].set(DEEPNORM_BETA).at[3 * A + 2 * R:3 * A + 3 * R].set(DEEPNORM_BETA)
    return {
        "x_prompt": nrm(ks[1], (BATCH, SEQ, D_MODEL), 1.0),
        "x_sample": nrm(ks[2], (DEC_BATCH, DEC_SEQ, D_MODEL), 1.0),
        "cache_k": jax.random.normal(ks[3], (DEPTH, n_pool, PAGE_SIZE, N_ATTN_HEADS, HEAD_DIM), jnp.float32),
        "cache_v": nrm(ks[4], (DEPTH, n_pool, PAGE_SIZE, N_ATTN_HEADS, HEAD_DIM), DEEPNORM_BETA),
        "state_ret": nrm(ks[5], (DEPTH, DEC_BATCH, N_RET_HEADS, HEAD_DIM, HEAD_DIM), 1.0),
        "state_conv": nrm(ks[6], (DEPTH, DEC_BATCH, CONV_W - 1, D_FF), DEEPNORM_BETA),
        "page_table": page_table,
        "p_prompt": nrm(ks[7], (DEPTH, BATCH, SEQ, PLE_DIM), 1.0),
        "p_sample": nrm(ks[8], (DEPTH, DEC_BATCH, DEC_SEQ, PLE_DIM), 1.0),
        "rel_bias": nrm(ks[9], (NUM_BUCKETS, N_ATTN_HEADS), 0.5),
        "w_in": nrm(ks[10], (DEPTH, D_MODEL, IN_WIDTH), D_MODEL ** -0.5) * col_scale,
        "w_out": nrm(ks[11], (DEPTH, MIX_WIDTH, D_MODEL), DEEPNORM_BETA * MIX_WIDTH ** -0.5),
        "ln1_g": 1.0 + nrm(ks[12], (DEPTH, D_MODEL), 0.02),
        "ln1_b": nrm(ks[13], (DEPTH, D_MODEL), 0.02),
        "w_gate": nrm(ks[14], (DEPTH, D_MODEL, D_FF), DEEPNORM_BETA * D_MODEL ** -0.5),
        "w_up": nrm(ks[15], (DEPTH, D_MODEL, D_FF), DEEPNORM_BETA * D_MODEL ** -0.5),
        "conv_w": nrm(ks[16], (DEPTH, CONV_W, D_FF), CONV_W ** -0.5),
        "conv_b": nrm(ks[17], (DEPTH, D_FF), 0.02),
        "w_down": nrm(ks[18], (DEPTH, D_FF, D_MODEL), DEEPNORM_BETA * D_FF ** -0.5),
        "ln2_g": 1.0 + nrm(ks[19], (DEPTH, D_MODEL), 0.02),
        "ln2_b": nrm(ks[20], (DEPTH, D_MODEL), 0.02),
        "w_ple": nrm(ks[21], (DEPTH, PLE_DIM, D_MODEL), PLE_DIM ** -0.5),
        "w_ple_gate": nrm(ks[22], (DEPTH, D_MODEL, D_MODEL), D_MODEL ** -0.5),
    }


def reference(x_prompt, x_sample, cache_k, cache_v, state_ret, state_conv, page_table, p_prompt, p_sample,
              rel_bias, w_in, w_out, ln1_g, ln1_b, w_gate, w_up, conv_w, conv_b, w_down, ln2_g, ln2_b,
              w_ple, w_ple_gate):
    bp = x_prompt.shape[0]
    bs = x_sample.shape[0]
    past_len = page_table.shape[1] * cache_k.shape[2]
    empty_kv = jnp.zeros((bp, 0, N_ATTN_HEADS, HEAD_DIM), cache_k.dtype)
    s0_prompt = jnp.zeros((bp, N_RET_HEADS, HEAD_DIM, HEAD_DIM), jnp.float32)
    conv0_prompt = jnp.zeros((bp, CONV_W - 1, D_FF), x_prompt.dtype)
    yp, ys = x_prompt, x_sample
    kp_l, vp_l, ks_l, vs_l, rp_l, rs_l, cp_l, cs_l = [], [], [], [], [], [], [], []
    for l in range(DEPTH):
        lw = (rel_bias, w_in[l], w_out[l], ln1_g[l], ln1_b[l], w_gate[l], w_up[l], conv_w[l], conv_b[l],
              w_down[l], ln2_g[l], ln2_b[l], w_ple[l], w_ple_gate[l])
        yp, kp, vp, rp, cp = decoder_layer(yp, p_prompt[l], empty_kv, empty_kv, s0_prompt, conv0_prompt, *lw)
        past_k = cache_k[l][page_table].reshape(bs, past_len, N_ATTN_HEADS, HEAD_DIM)
        past_v = cache_v[l][page_table].reshape(bs, past_len, N_ATTN_HEADS, HEAD_DIM)
        ys, ksm, vsm, rsm, csm = decoder_layer(ys, p_sample[l], past_k, past_v, state_ret[l], state_conv[l], *lw)
        kp_l.append(kp); vp_l.append(vp); ks_l.append(ksm); vs_l.append(vsm)
        rp_l.append(rp); rs_l.append(rsm); cp_l.append(cp); cs_l.append(csm)
    k_prompt = jnp.stack(kp_l)
    v_prompt = jnp.stack(vp_l)
    k_sample = jnp.stack(ks_l)
    v_sample = jnp.stack(vs_l)
    ret_prompt = jnp.stack(rp_l)
    ret_sample = jnp.stack(rs_l)
    conv_prompt = jnp.stack(cp_l)
    conv_sample = jnp.stack(cs_l)
    return (yp, ys, k_prompt, v_prompt, k_sample, v_sample, ret_prompt, ret_sample, conv_prompt, conv_sample)
```

```python
import functools
import math

import jax
import jax.numpy as jnp
import numpy as np
from jax import lax
from jax.experimental import pallas as pl
from jax.experimental.pallas import tpu as pltpu

HEAD_DIM = 128
MOBA_BLOCK = 256
MOBA_TOPK = 3
RET_CHUNK = 128
ROPE_BASE = 10000.0
NUM_BUCKETS = 32
MAX_DISTANCE = 128
CONV_W = 3
LN_EPS = 1e-5

VMEM_LIMIT_BYTES = 56 * 1024 * 1024
NEG_BIG = -1e30

BF16 = jnp.bfloat16
F32 = jnp.float32

_NT = (((1,), (1,)), ((), ()))
_TN = (((0,), (0,)), ((), ()))


def _cparams(*sem):
    return pltpu.CompilerParams(dimension_semantics=sem, vmem_limit_bytes=VMEM_LIMIT_BYTES)


def _sigmoid(x):
    return 1.0 / (1.0 + jnp.exp(-x))


def _layer_norm_rows(y, g, b):
    mu = jnp.mean(y, axis=-1, keepdims=True)
    d = y - mu
    var = jnp.mean(d * d, axis=-1, keepdims=True)
    return d * lax.rsqrt(var + LN_EPS) * g + b


def _split_bf16(a):
    hi = a.astype(BF16)
    lo = (a - hi.astype(F32)).astype(BF16)
    return hi, lo


def _dot_nt_precise(a, b):
    ah, al = _split_bf16(a)
    bh, bl = _split_bf16(b)
    dot = functools.partial(lax.dot_general, dimension_numbers=_NT, preferred_element_type=F32)
    return dot(ah, bh) + (dot(ah, bl) + dot(al, bh))


def _mm_kernel(x_ref, w_ref, o_ref):
    o_ref[...] = jnp.dot(x_ref[...], w_ref[0], preferred_element_type=F32).astype(o_ref.dtype)


def _matmul(x, w, layer, *, tm, tn, out_dtype=F32):
    m, k = x.shape
    n = w.shape[2]
    return pl.pallas_call(
        _mm_kernel,
        grid=(m // tm, n // tn),
        in_specs=[
            pl.BlockSpec((tm, k), lambda i, j: (i, 0)),
            pl.BlockSpec((1, k, tn), lambda i, j: (layer, 0, j)),
        ],
        out_specs=pl.BlockSpec((tm, tn), lambda i, j: (i, j)),
        out_shape=jax.ShapeDtypeStruct((m, n), out_dtype),
        compiler_params=_cparams("parallel", "arbitrary"),
        name="proj_in",
    )(x, w)


def _moba_prompt_kernel(q_ref, k_ref, v_ref, bd_ref, ba_ref, bf_ref, o_ref,
                        kb_ref, vb_ref, km_ref, sel_ref, m_ref, l_ref, acc_ref):
    c = pl.program_id(2)
    tq = q_ref.shape[0]
    nb = k_ref.shape[0] // MOBA_BLOCK
    scale = HEAD_DIM ** -0.5

    @pl.when(c == 0)
    def _():
        kb_ref[...] = k_ref[...].astype(BF16)
        vb_ref[...] = v_ref[...].astype(BF16)
        km_ref[...] = jnp.zeros_like(km_ref)
        for n in range(nb):
            km_ref[n:n + 1, :] = jnp.mean(k_ref[n * MOBA_BLOCK:(n + 1) * MOBA_BLOCK, :], axis=0, keepdims=True)

    qf = q_ref[...]
    qb = qf.astype(BF16)

    nbp = km_ref.shape[0]
    gate = _dot_nt_precise(km_ref[...], qf)
    blk = lax.broadcasted_iota(jnp.int32, (nbp, tq), 0)
    eligible = blk < c
    gate = jnp.where(eligible, gate, -jnp.inf)
    rank = jnp.zeros((nbp, tq), F32)
    for mth in range(nb):
        gm = gate[mth:mth + 1, :]
        beats = jnp.logical_or(gm > gate, jnp.logical_and(gm == gate, mth < blk))
        rank = rank + jnp.where(jnp.logical_and(beats, mth < c), 1.0, 0.0)
    sel_t = jnp.where(jnp.logical_and(eligible, rank < MOBA_TOPK), 1.0, 0.0)
    sel_pad = jnp.concatenate([sel_t, jnp.zeros((128 - nbp, tq), F32)], axis=0).astype(BF16)
    rows = lax.broadcasted_iota(jnp.int32, (tq, tq), 0)
    cols = lax.broadcasted_iota(jnp.int32, (tq, tq), 1)
    eye = jnp.where(rows == cols, 1.0, 0.0).astype(BF16)
    sel_ref[...] = lax.dot_general(eye, sel_pad, _NT, preferred_element_type=F32)

    start = pl.multiple_of(c * MOBA_BLOCK, MOBA_BLOCK)
    ks = kb_ref[pl.ds(start, MOBA_BLOCK), :]
    vs = vb_ref[pl.ds(start, MOBA_BLOCK), :]
    s = lax.dot_general(qb, ks, _NT, preferred_element_type=F32) * scale + bd_ref[0]
    s = jnp.where(cols <= rows, s, NEG_BIG)
    m0 = jnp.max(s, axis=-1, keepdims=True)
    p = jnp.exp(s - m0)
    m_ref[...] = m0
    l_ref[...] = jnp.sum(p, axis=-1, keepdims=True)
    acc_ref[...] = jnp.dot(p.astype(BF16), vs, preferred_element_type=F32)

    lane = lax.broadcasted_iota(jnp.int32, (tq, 128), 1)

    def past_block(j, bias):
        st = pl.multiple_of(j * MOBA_BLOCK, MOBA_BLOCK)
        kj = kb_ref[pl.ds(st, MOBA_BLOCK), :]
        vj = vb_ref[pl.ds(st, MOBA_BLOCK), :]
        flag = jnp.sum(jnp.where(lane == j, sel_ref[...], 0.0), axis=-1, keepdims=True)
        sj = lax.dot_general(qb, kj, _NT, preferred_element_type=F32) * scale + bias
        sj = jnp.where(flag > 0.5, sj, NEG_BIG)
        m_old = m_ref[...]
        m_new = jnp.maximum(m_old, jnp.max(sj, axis=-1, keepdims=True))
        alpha = jnp.exp(m_old - m_new)
        pj = jnp.exp(sj - m_new)
        m_ref[...] = m_new
        l_ref[...] = alpha * l_ref[...] + jnp.sum(pj, axis=-1, keepdims=True)
        acc_ref[...] = alpha * acc_ref[...] + jnp.dot(pj.astype(BF16), vj, preferred_element_type=F32)

    @pl.when(c >= 1)
    def _():
        past_block(c - 1, ba_ref[0])

    far_bias = bf_ref[0, 0:1, 0:1]

    def far_body(j, carry):
        past_block(j, far_bias)
        return carry

    lax.fori_loop(0, jnp.maximum(c - 1, 0), far_body, 0)
    o_ref[...] = (acc_ref[...] / l_ref[...]).astype(o_ref.dtype)


def _moba_prompt(h, bias_diag, bias_adj, bias_far, *, batch, seq, n_heads):
    nq = seq // MOBA_BLOCK
    nbias = bias_far.shape[1]
    return pl.pallas_call(
        _moba_prompt_kernel,
        grid=(batch, n_heads, nq),
        in_specs=[
            pl.BlockSpec((MOBA_BLOCK, HEAD_DIM), lambda b, hh, qi: (b * nq + qi, hh)),
            pl.BlockSpec((seq, HEAD_DIM), lambda b, hh, qi: (b, n_heads + hh)),
            pl.BlockSpec((seq, HEAD_DIM), lambda b, hh, qi: (b, 2 * n_heads + hh)),
            pl.BlockSpec((1, MOBA_BLOCK, MOBA_BLOCK), lambda b, hh, qi: (hh, 0, 0)),
            pl.BlockSpec((1, MOBA_BLOCK, MOBA_BLOCK), lambda b, hh, qi: (hh, 0, 0)),
            pl.BlockSpec((1, nbias, 128), lambda b, hh, qi: (hh, 0, 0)),
        ],
        out_specs=pl.BlockSpec((MOBA_BLOCK, HEAD_DIM), lambda b, hh, qi: (b * nq + qi, hh)),
        out_shape=jax.ShapeDtypeStruct((batch * seq, n_heads * HEAD_DIM), BF16),
        scratch_shapes=[
            pltpu.VMEM((seq, HEAD_DIM), BF16),
            pltpu.VMEM((seq, HEAD_DIM), BF16),
            pltpu.VMEM((max(16, seq // MOBA_BLOCK), HEAD_DIM), F32),
            pltpu.VMEM((MOBA_BLOCK, 128), F32),
            pltpu.VMEM((MOBA_BLOCK, 1), F32),
            pltpu.VMEM((MOBA_BLOCK, 1), F32),
            pltpu.VMEM((MOBA_BLOCK, HEAD_DIM), F32),
        ],
        compiler_params=_cparams("parallel", "parallel", "arbitrary"),
        name="moba_prompt",
    )(h, h, h, bias_diag, bias_adj, bias_far)


def _rotate(x, cos, sin_signed):
    return x * cos + pltpu.roll(x, HEAD_DIM // 2, 1) * sin_signed


def _head_norm_gate(o, g):
    mu = jnp.mean(o, axis=-1, keepdims=True)
    d = o - mu
    var = jnp.mean(d * d, axis=-1, keepdims=True)
    return g * _sigmoid(g) * (d * lax.rsqrt(var + LN_EPS))


def _ret_prompt_kernel(q_ref, k_ref, v_ref, g_ref, cos_ref, sin_ref, dm_ref, cr_ref, up_ref, cd_ref,
                       s0_ref, o_ref, s_ref):
    n_heads = s_ref.shape[1]
    scale = HEAD_DIM ** -0.5

    @pl.when(pl.program_id(1) == 0)
    def _():
        s_ref[...] = s0_ref[...]

    cos = cos_ref[...]
    sin = sin_ref[...]
    for hh in range(n_heads):
        sl = slice(hh * HEAD_DIM, (hh + 1) * HEAD_DIM)
        q = _rotate(q_ref[:, sl], cos, sin)
        k = _rotate(k_ref[:, sl], cos, sin) * scale
        qb = q.astype(BF16)
        kb = k.astype(BF16)
        vb = v_ref[:, sl].astype(BF16)
        s = s_ref[0, hh]
        sc = lax.dot_general(qb, kb, _NT, preferred_element_type=F32) * dm_ref[hh]
        o = jnp.dot(sc.astype(BF16), vb, preferred_element_type=F32)
        o = o + jnp.dot(qb, s.astype(BF16), preferred_element_type=F32) * cr_ref[hh]
        ku = (k * up_ref[hh]).astype(BF16)
        s_ref[0, hh] = s * cd_ref[hh] + lax.dot_general(ku, vb, _TN, preferred_element_type=F32)
        o_ref[:, sl] = _head_norm_gate(o, g_ref[:, sl]).astype(o_ref.dtype)


def _ret_prompt(h, cos, sin, tabs, s0, *, batch, seq, n_heads, col0):
    nc = seq // RET_CHUNK
    width = n_heads * HEAD_DIM
    hspec = lambda g: pl.BlockSpec((RET_CHUNK, width), lambda b, c: (b * nc + c, col0 + g))
    tspec = pl.BlockSpec((n_heads, RET_CHUNK, HEAD_DIM), lambda b, c: (0, 0, 0))
    sspec = pl.BlockSpec((1, n_heads, HEAD_DIM, HEAD_DIM), lambda b, c: (b, 0, 0, 0))
    return pl.pallas_call(
        _ret_prompt_kernel,
        grid=(batch, nc),
        in_specs=[hspec(0), hspec(1), hspec(2), hspec(3),
                  pl.BlockSpec((RET_CHUNK, HEAD_DIM), lambda b, c: (c, 0)),
                  pl.BlockSpec((RET_CHUNK, HEAD_DIM), lambda b, c: (c, 0)),
                  tspec, tspec, tspec, tspec, sspec],
        out_specs=[pl.BlockSpec((RET_CHUNK, width), lambda b, c: (b * nc + c, 0)), sspec],
        out_shape=[jax.ShapeDtypeStruct((batch * seq, width), BF16),
                   jax.ShapeDtypeStruct((batch, n_heads, HEAD_DIM, HEAD_DIM), F32)],
        compiler_params=_cparams("parallel", "arbitrary"),
        name="ret_prompt",
    )(h, h, h, h, cos, sin, *tabs, s0)


def _outproj_kernel(alpha, a_ref, r_ref, wa_ref, wr_ref, x_ref, g_ref, b_ref, of_ref, ob_ref):
    mix = jnp.dot(a_ref[...], wa_ref[0], preferred_element_type=F32)
    mix = mix + jnp.dot(r_ref[...], wr_ref[0], preferred_element_type=F32)
    y = _layer_norm_rows(alpha * x_ref[...] + mix, g_ref[0], b_ref[0])
    of_ref[...] = y
    ob_ref[...] = y.astype(BF16)


def _outproj_ln(attn, ret, w_out, layer, x, g, b, *, alpha, tm):
    m, ka = attn.shape
    d = x.shape[1]
    row = lambda width: pl.BlockSpec((tm, width), lambda i: (i, 0))
    vec = pl.BlockSpec((1, 1, d), lambda i: (layer, 0, 0))
    return pl.pallas_call(
        functools.partial(_outproj_kernel, alpha),
        grid=(m // tm,),
        in_specs=[row(ka), row(ka),
                  pl.BlockSpec((1, ka, d), lambda i: (layer, 0, 0)),
                  pl.BlockSpec((1, ka, d), lambda i: (layer, 1, 0)),
                  row(d), vec, vec],
        out_specs=[row(d), row(d)],
        out_shape=[jax.ShapeDtypeStruct((m, d), F32), jax.ShapeDtypeStruct((m, d), BF16)],
        compiler_params=_cparams("parallel"),
        name="outproj_ln",
    )(attn, ret, w_out, w_out, x, g, b)


def _ffn_up_prompt_kernel(tiles_per_seq, x_ref, wg_ref, wu_ref, cw_ref, cb_ref, buf_ref,
                          o_ref, cn_ref, carry_ref):
    i = pl.program_id(1)
    tm = x_ref.shape[0]
    x = x_ref[...]
    a = jnp.dot(x, wg_ref[0], preferred_element_type=F32)
    u = jnp.dot(x, wu_ref[0], preferred_element_type=F32)
    first = (i % tiles_per_seq) == 0
    prev = jnp.where(first, buf_ref[0], carry_ref[6:8, :])
    row = lax.broadcasted_iota(jnp.int32, a.shape, 0)
    a1 = jnp.where(row == 0, prev[1:2, :], pltpu.roll(a, 1, 0))
    a2 = jnp.where(row == 0, prev[0:1, :], jnp.where(row == 1, prev[1:2, :], pltpu.roll(a, 2, 0)))
    cw = cw_ref[0]
    conv = cb_ref[0] + cw[0:1, :] * a2 + cw[1:2, :] * a1 + cw[2:3, :] * a
    o_ref[...] = (conv * _sigmoid(conv) * u).astype(o_ref.dtype)
    carry_ref[...] = a[tm - 8:tm, :]

    @pl.when((i % tiles_per_seq) == tiles_per_seq - 1)
    def _():
        cn_ref[0] = a[tm - (CONV_W - 1):tm, :]


def _ffn_up_prompt(x, w_gate, w_up, conv_w, conv_b, conv_buf, layer, *, batch, seq, tm, tn):
    m, d = x.shape
    nf = w_gate.shape[2]
    tps = seq // tm
    wspec = pl.BlockSpec((1, d, tn), lambda j, i: (layer, 0, j))
    return pl.pallas_call(
        functools.partial(_ffn_up_prompt_kernel, tps),
        grid=(nf // tn, m // tm),
        in_specs=[pl.BlockSpec((tm, d), lambda j, i: (i, 0)), wspec, wspec,
                  pl.BlockSpec((1, CONV_W, tn), lambda j, i: (layer, 0, j)),
                  pl.BlockSpec((1, 1, tn), lambda j, i: (layer, 0, j)),
                  pl.BlockSpec((1, CONV_W - 1, tn), lambda j, i: (i // tps, 0, j))],
        out_specs=[pl.BlockSpec((tm, tn), lambda j, i: (i, j)),
                   pl.BlockSpec((1, CONV_W - 1, tn), lambda j, i: (i // tps, 0, j))],
        out_shape=[jax.ShapeDtypeStruct((m, nf), BF16),
                   jax.ShapeDtypeStruct((batch, CONV_W - 1, nf), F32)],
        scratch_shapes=[pltpu.VMEM((8, tn), F32)],
        compiler_params=_cparams("parallel", "arbitrary"),
        name="ffn_up_prompt",
    )(x, w_gate, w_up, conv_w, conv_b, conv_buf)


def _ffn_up_sample_kernel(x_ref, wg_ref, wu_ref, cw_ref, cb_ref, b0_ref, b1_ref, o_ref, n0_ref, n1_ref):
    x = x_ref[...]
    a = jnp.dot(x, wg_ref[0], preferred_element_type=F32)
    u = jnp.dot(x, wu_ref[0], preferred_element_type=F32)
    cw = cw_ref[0]
    conv = cb_ref[0] + cw[0:1, :] * b0_ref[...] + cw[1:2, :] * b1_ref[...] + cw[2:3, :] * a
    o_ref[...] = (conv * _sigmoid(conv) * u).astype(o_ref.dtype)
    n0_ref[...] = b1_ref[...]
    n1_ref[...] = a


def _ffn_up_sample(x, w_gate, w_up, conv_w, conv_b, buf0, buf1, layer, *, tn):
    m, d = x.shape
    nf = w_gate.shape[2]
    wspec = pl.BlockSpec((1, d, tn), lambda j: (layer, 0, j))
    cspec = pl.BlockSpec((m, tn), lambda j: (0, j))
    return pl.pallas_call(
        _ffn_up_sample_kernel,
        grid=(nf // tn,),
        in_specs=[pl.BlockSpec((m, d), lambda j: (0, 0)), wspec, wspec,
                  pl.BlockSpec((1, CONV_W, tn), lambda j: (layer, 0, j)),
                  pl.BlockSpec((1, 1, tn), lambda j: (layer, 0, j)),
                  cspec, cspec],
        out_specs=[cspec, cspec, cspec],
        out_shape=[jax.ShapeDtypeStruct((m, nf), BF16),
                   jax.ShapeDtypeStruct((m, nf), F32),
                   jax.ShapeDtypeStruct((m, nf), F32)],
        compiler_params=_cparams("parallel"),
        name="ffn_up_sample",
    )(x, w_gate, w_up, conv_w, conv_b, buf0, buf1)


def _ffn_down_kernel(alpha, g_ref, w_ref, x_ref, lg_ref, lb_ref, o_ref, acc_ref):
    kk = pl.program_id(1)

    @pl.when(kk == 0)
    def _():
        acc_ref[...] = jnp.zeros_like(acc_ref)

    acc_ref[...] += jnp.dot(g_ref[...], w_ref[0], preferred_element_type=F32)

    @pl.when(kk == pl.num_programs(1) - 1)
    def _():
        o_ref[...] = _layer_norm_rows(alpha * x_ref[...] + acc_ref[...], lg_ref[0], lb_ref[0])


def _ffn_down_ln(g, w_down, layer, x, ln_g, ln_b, *, alpha, tm, tk):
    m, nf = g.shape
    d = x.shape[1]
    vec = pl.BlockSpec((1, 1, d), lambda i, kk: (layer, 0, 0))
    return pl.pallas_call(
        functools.partial(_ffn_down_kernel, alpha),
        grid=(m // tm, nf // tk),
        in_specs=[pl.BlockSpec((tm, tk), lambda i, kk: (i, kk)),
                  pl.BlockSpec((1, tk, d), lambda i, kk: (layer, kk, 0)),
                  pl.BlockSpec((tm, d), lambda i, kk: (i, 0)), vec, vec],
        out_specs=pl.BlockSpec((tm, d), lambda i, kk: (i, 0)),
        out_shape=jax.ShapeDtypeStruct((m, d), F32),
        scratch_shapes=[pltpu.VMEM((tm, d), F32)],
        compiler_params=_cparams("parallel", "arbitrary"),
        name="ffn_down_ln",
    )(g, w_down, x, ln_g, ln_b)


def _ple_kernel(x_ref, p_ref, wg_ref, wp_ref, of_ref, ob_ref):
    x = x_ref[...]
    gate = _sigmoid(jnp.dot(x.astype(BF16), wg_ref[0], preferred_element_type=F32))
    emb = jnp.dot(p_ref[...].astype(BF16), wp_ref[0], preferred_element_type=F32)
    y = x + gate * emb
    of_ref[...] = y
    ob_ref[...] = y.astype(BF16)


def _ple(x, p, w_ple_gate, w_ple, layer, *, tm):
    m, d = x.shape
    pd = p.shape[1]
    row = lambda width: pl.BlockSpec((tm, width), lambda i: (i, 0))
    return pl.pallas_call(
        _ple_kernel,
        grid=(m // tm,),
        in_specs=[row(d), row(pd),
                  pl.BlockSpec((1, d, d), lambda i: (layer, 0, 0)),
                  pl.BlockSpec((1, pd, d), lambda i: (layer, 0, 0))],
        out_specs=[row(d), row(d)],
        out_shape=[jax.ShapeDtypeStruct((m, d), F32), jax.ShapeDtypeStruct((m, d), BF16)],
        compiler_params=_cparams("parallel"),
        name="ple",
    )(x, p, w_ple_gate, w_ple)


def _block_ksum_kernel(ids_ref, p0_ref, p1_ref, o_ref):
    del ids_ref
    o_ref[0] = jnp.sum(p0_ref[0], axis=0, keepdims=True) + jnp.sum(p1_ref[0], axis=0, keepdims=True)


def _block_ksum(cache_k_pages, page_ids, pages_per_block):
    assert pages_per_block == 2
    _, page, width = cache_k_pages.shape
    n_blocks = page_ids.shape[0] // pages_per_block
    return pl.pallas_call(
        _block_ksum_kernel,
        grid_spec=pltpu.PrefetchScalarGridSpec(
            num_scalar_prefetch=1,
            grid=(n_blocks,),
            in_specs=[pl.BlockSpec((1, page, width), lambda n, ids: (ids[2 * n], 0, 0)),
                      pl.BlockSpec((1, page, width), lambda n, ids: (ids[2 * n + 1], 0, 0))],
            out_specs=pl.BlockSpec((1, 1, width), lambda n, ids: (n, 0, 0)),
        ),
        out_shape=jax.ShapeDtypeStruct((n_blocks, 1, width), F32),
        compiler_params=_cparams("parallel"),
        name="block_ksum",
    )(page_ids, cache_k_pages, cache_k_pages)


def _sample_select_kernel(q_ref, ks_ref, o_ref):
    nblk, width = ks_ref.shape[1], ks_ref.shape[2]
    n_heads = width // HEAD_DIM
    q = q_ref[0]
    prod = ks_ref[0] * q * (1.0 / MOBA_BLOCK)
    lane = lax.broadcasted_iota(jnp.int32, (nblk, 128), 1)
    gate = jnp.full((nblk, 128), -jnp.inf, F32)
    for hh in range(n_heads):
        gh = jnp.sum(prod[:, hh * HEAD_DIM:(hh + 1) * HEAD_DIM], axis=-1, keepdims=True)
        gate = jnp.where(lane == hh, gh, gate)
    blk = lax.broadcasted_iota(jnp.int32, (nblk, 128), 0).astype(F32)
    picks = []
    for _ in range(MOBA_TOPK):
        best = jnp.max(gate, axis=0, keepdims=True)
        idx = jnp.min(jnp.where(gate == best, blk, float(nblk)), axis=0, keepdims=True)
        picks.append(idx)
        gate = jnp.where(blk == idx, -jnp.inf, gate)
    picks.append(jnp.zeros((8 - MOBA_TOPK, 128), F32))
    o_ref[0] = jnp.concatenate(picks, axis=0).astype(jnp.int32)


def _sample_select(hs, ksum_l, *, n_heads):
    bs, nblk, width = ksum_l.shape
    return pl.pallas_call(
        _sample_select_kernel,
        grid=(bs,),
        in_specs=[pl.BlockSpec((1, 1, width), lambda b: (b, 0, 0)),
                  pl.BlockSpec((1, nblk, width), lambda b: (b, 0, 0))],
        out_specs=pl.BlockSpec((1, 8, 128), lambda b: (b, 0, 0)),
        out_shape=jax.ShapeDtypeStruct((bs, 8, 128), jnp.int32),
        compiler_params=_cparams("parallel"),
        name="sample_select",
    )(hs, ksum_l)


def _sample_attn_kernel(pid_ref, near_ref, q_ref, kn_ref, vn_ref, kp_ref, vp_ref, bn_ref, bf_ref,
                        o_ref, m_ref, l_ref, acc_ref):
    del pid_ref
    b = pl.program_id(0)
    hh = pl.program_id(1)
    j = pl.program_id(2)
    npg = pl.num_programs(2)
    scale = HEAD_DIM ** -0.5
    q = q_ref[0]
    qb = jnp.broadcast_to(q, (8, HEAD_DIM)).astype(BF16)

    @pl.when(j == 0)
    def _():
        kn = kn_ref[0]
        s0 = jnp.sum(q.astype(BF16).astype(F32) * kn.astype(BF16).astype(F32), axis=-1, keepdims=True)
        m_ref[...] = s0 * scale + bf_ref[0, 1:2, 0:1]
        l_ref[...] = jnp.ones_like(l_ref)
        acc_ref[...] = vn_ref[0].astype(BF16).astype(F32)

    kp = kp_ref[0].astype(BF16)
    vp = vp_ref[0].astype(BF16)
    s = lax.dot_general(qb, kp, _NT, preferred_element_type=F32)[0:1, :] * scale
    near = near_ref[(b * pl.num_programs(1) + hh) * npg + j]
    half = lax.rem(j, 2)
    bias_near = jnp.where(half == 0, bn_ref[0, 0:1, :], bn_ref[0, 1:2, :])
    s = s + jnp.where(near == 1, bias_near, bf_ref[0, 0:1, 0:1])
    m_old = m_ref[...]
    m_new = jnp.maximum(m_old, jnp.max(s, axis=-1, keepdims=True))
    alpha = jnp.exp(m_old - m_new)
    p = jnp.exp(s - m_new)
    m_ref[...] = m_new
    l_ref[...] = alpha * l_ref[...] + jnp.sum(p, axis=-1, keepdims=True)
    pv = jnp.dot(jnp.broadcast_to(p, (8, p.shape[1])).astype(BF16), vp, preferred_element_type=F32)[0:1, :]
    acc_ref[...] = alpha * acc_ref[...] + pv

    @pl.when(j == npg - 1)
    def _():
        o_ref[0, 0] = (acc_ref[...] / l_ref[...]).astype(o_ref.dtype)


def _sample_attn(hs, cache_k_pages, cache_v_pages, page_ids, near, bias_near, bias_far, *, n_heads, n_sel_pages):
    bs = hs.shape[0]
    _, page, _ = cache_k_pages.shape
    pspec = pl.BlockSpec((1, page, HEAD_DIM),
                         lambda b, hh, j, pid, nr: (pid[(b * n_heads + hh) * n_sel_pages + j], 0, hh))
    hspec = lambda g: pl.BlockSpec((1, 1, HEAD_DIM), lambda b, hh, j, pid, nr: (b, 0, g * n_heads + hh))
    return pl.pallas_call(
        _sample_attn_kernel,
        grid_spec=pltpu.PrefetchScalarGridSpec(
            num_scalar_prefetch=2,
            grid=(bs, n_heads, n_sel_pages),
            in_specs=[hspec(0), hspec(1), hspec(2), pspec, pspec,
                      pl.BlockSpec((1, 2, page), lambda b, hh, j, pid, nr: (hh, 0, 0)),
                      pl.BlockSpec((1, 2, 128), lambda b, hh, j, pid, nr: (hh, 0, 0))],
            out_specs=pl.BlockSpec((1, 1, 1, HEAD_DIM), lambda b, hh, j, pid, nr: (b, hh, 0, 0)),
            scratch_shapes=[pltpu.VMEM((1, 1), F32), pltpu.VMEM((1, 1), F32), pltpu.VMEM((1, HEAD_DIM), F32)],
        ),
        out_shape=jax.ShapeDtypeStruct((bs, n_heads, 1, HEAD_DIM), BF16),
        compiler_params=_cparams("parallel", "parallel", "arbitrary"),
        name="sample_attn",
    )(page_ids, near, hs, hs, hs, cache_k_pages, cache_v_pages, bias_near, bias_far)


def _ret_sample_kernel(q_ref, k_ref, v_ref, g_ref, cos_ref, sin_ref, dec_ref, s0_ref, o_ref, s_ref):
    n_heads = s_ref.shape[1]
    scale = HEAD_DIM ** -0.5
    cos = cos_ref[...]
    sin = sin_ref[...]
    rows = lax.broadcasted_iota(jnp.int32, (HEAD_DIM, HEAD_DIM), 0)
    cols = lax.broadcasted_iota(jnp.int32, (HEAD_DIM, HEAD_DIM), 1)
    eye = jnp.where(rows == cols, 1.0, 0.0)
    for hh in range(n_heads):
        sl = slice(hh * HEAD_DIM, (hh + 1) * HEAD_DIM)
        q = _rotate(q_ref[0, :, sl], cos, sin)
        k = _rotate(k_ref[0, :, sl], cos, sin) * scale
        v = v_ref[0, :, sl]
        g = g_ref[0, :, sl]
        dec = dec_ref[hh:hh + 1, :]
        q_col = jnp.sum(eye * q, axis=-1, keepdims=True)
        k_col = jnp.sum(eye * k, axis=-1, keepdims=True)
        s = s0_ref[0, hh]
        qk = jnp.sum(q * k, axis=-1, keepdims=True)
        o = qk * v + jnp.sum(q_col * s, axis=0, keepdims=True) * dec
        s_ref[0, hh] = s * dec + k_col * v
        o_ref[0, :, sl] = _head_norm_gate(o, g).astype(o_ref.dtype)


def _ret_sample(hs, cos, sin, decay, s0, *, n_heads, col0):
    bs = hs.shape[0]
    width = n_heads * HEAD_DIM
    hspec = lambda g: pl.BlockSpec((1, 1, width), lambda b: (b, 0, col0 + g))
    vspec = pl.BlockSpec((1, HEAD_DIM), lambda b: (0, 0))
    sspec = pl.BlockSpec((1, n_heads, HEAD_DIM, HEAD_DIM), lambda b: (b, 0, 0, 0))
    return pl.pallas_call(
        _ret_sample_kernel,
        grid=(bs,),
        in_specs=[hspec(0), hspec(1), hspec(2), hspec(3), vspec, vspec,
                  pl.BlockSpec((n_heads, HEAD_DIM), lambda b: (0, 0)), sspec],
        out_specs=[pl.BlockSpec((1, 1, width), lambda b: (b, 0, 0)), sspec],
        out_shape=[jax.ShapeDtypeStruct((bs, 1, width), BF16),
                   jax.ShapeDtypeStruct((bs, n_heads, HEAD_DIM, HEAD_DIM), F32)],
        compiler_params=_cparams("parallel"),
        name="ret_sample",
    )(hs, hs, hs, hs, cos, sin, decay, s0)


def _t5_bucket(rel):
    n = jnp.maximum(rel, 0)
    max_exact = NUM_BUCKETS // 2
    nf = jnp.maximum(n, max_exact).astype(F32)
    large = max_exact + (jnp.log(nf / max_exact) / math.log(MAX_DISTANCE / max_exact)
                         * (NUM_BUCKETS - max_exact)).astype(jnp.int32)
    return jnp.where(n < max_exact, n, jnp.minimum(large, NUM_BUCKETS - 1))


def _rope_tables(pos):
    half = HEAD_DIM // 2
    inv = ROPE_BASE ** (-jnp.arange(half, dtype=F32) / half)
    ang = pos[:, None] * inv[None, :]
    cos = jnp.cos(ang)
    sin = jnp.sin(ang)
    return jnp.concatenate([cos, cos], -1), jnp.concatenate([-sin, sin], -1)


def _decay_tables(n_heads, chunk):
    lg = jnp.log(1.0 - 2.0 ** (-5.0 - jnp.arange(n_heads, dtype=F32)))
    i = jnp.arange(chunk, dtype=F32)
    diff = i[:, None] - i[None, :]
    dmask = jnp.where(diff >= 0, jnp.exp(lg[:, None, None] * jnp.maximum(diff, 0.0)), 0.0)
    full = lambda col: jnp.broadcast_to(col[..., None], (n_heads, chunk, HEAD_DIM))
    cross = full(jnp.exp(lg[:, None] * (i + 1.0)))
    upd = full(jnp.exp(lg[:, None] * (chunk - 1.0 - i)))
    cdec = jnp.broadcast_to(jnp.exp(lg * chunk)[:, None, None], (n_heads, chunk, HEAD_DIM))
    return dmask, cross, upd, cdec


def kernel(x_prompt, x_sample, cache_k, cache_v, state_ret, state_conv, page_table, p_prompt, p_sample,
           rel_bias, w_in, w_out, ln1_g, ln1_b, w_gate, w_up, conv_w, conv_b, w_down, ln2_g, ln2_b,
           w_ple, w_ple_gate):
    bp, seq, d = x_prompt.shape
    bs, dec_seq, _ = x_sample.shape
    depth, n_pool, page, n_heads, hd = cache_k.shape
    n_pages = page_table.shape[1]
    past_len = n_pages * page
    nf = w_gate.shape[2]
    attn_w = n_heads * hd
    assert hd == HEAD_DIM and dec_seq == 1 and MOBA_BLOCK % page == 0
    assert seq % MOBA_BLOCK == 0 and past_len % MOBA_BLOCK == 0 and past_len // MOBA_BLOCK >= MOBA_TOPK
    assert w_in.shape[2] == 7 * attn_w and d == 2 * attn_w and RET_CHUNK == HEAD_DIM
    assert int(np.floor(np.log(np.float32(MOBA_BLOCK + 1) / 16) / math.log(MAX_DISTANCE / 16) * 16)) >= 15
    alpha = (2 * depth) ** 0.25
    ppb = MOBA_BLOCK // page
    n_past_blocks = past_len // MOBA_BLOCK
    mp = bp * seq

    w_in_b, w_out_b, w_gate_b, w_up_b, w_down_b, w_ple_b, w_pg_b = (
        w.astype(BF16) for w in (w_in, w_out, w_gate, w_up, w_down, w_ple, w_ple_gate))
    vec3 = lambda a: a.reshape(depth, 1, a.shape[-1])
    ln1_g3, ln1_b3, ln2_g3, ln2_b3, conv_b3 = map(vec3, (ln1_g, ln1_b, ln2_g, ln2_b, conv_b))

    bias_by_rel = rel_bias[_t5_bucket(jnp.arange(2 * MOBA_BLOCK + 1))].T.astype(F32)
    ii = np.arange(MOBA_BLOCK)
    rel_diag = np.maximum(ii[:, None] - ii[None, :], 0)
    bias_diag = bias_by_rel[:, rel_diag]
    bias_adj = bias_by_rel[:, MOBA_BLOCK + ii[:, None] - ii[None, :]]
    far = rel_bias[NUM_BUCKETS - 1].astype(F32)
    bias_far = jnp.broadcast_to(far[:, None, None], (n_heads, 1, 128))
    bias_near_s = bias_by_rel[:, MOBA_BLOCK - np.arange(MOBA_BLOCK)].reshape(n_heads, ppb, page)
    bias_far_s = jnp.stack([jnp.broadcast_to(far[:, None], (n_heads, 128)),
                            jnp.broadcast_to(bias_by_rel[:, 0:1], (n_heads, 128))], axis=1)

    cos_p, sin_p = _rope_tables(jnp.arange(seq, dtype=F32))
    cos_s, sin_s = _rope_tables(jnp.full((1,), float(past_len), F32))
    tabs = _decay_tables(n_heads, RET_CHUNK)
    decay_s = jnp.broadcast_to(
        (1.0 - 2.0 ** (-5.0 - jnp.arange(n_heads, dtype=F32)))[:, None], (n_heads, HEAD_DIM))
    decay_s = jnp.exp(jnp.log(decay_s))

    ck = cache_k.reshape(depth * n_pool, page, attn_w)
    cv = cache_v.reshape(depth * n_pool, page, attn_w)
    page_ids = (jnp.arange(depth, dtype=jnp.int32)[:, None, None] * n_pool + page_table[None]).astype(jnp.int32)
    ksum = _block_ksum(ck, page_ids.reshape(-1), ppb)
    ksum = ksum.reshape(depth, bs, n_past_blocks, attn_w)

    xp_f = x_prompt.reshape(mp, d)
    xp_b = xp_f.astype(BF16)
    xs_f = x_sample.reshape(bs, d)
    xs_b = xs_f.astype(BF16)
    s0_prompt = jnp.zeros((bp, n_heads, hd, hd), F32)
    conv0_prompt = jnp.zeros((bp, CONV_W - 1, nf), F32)

    outs = {k: [] for k in ("kp", "vp", "ks", "vs", "rp", "rs", "cp", "cs")}
    for l in range(depth):
        hp = _matmul(xp_b, w_in_b, l, tm=1024, tn=1024)
        outs["kp"].append(hp[:, attn_w:2 * attn_w].reshape(bp, seq, n_heads, hd))
        outs["vp"].append(hp[:, 2 * attn_w:3 * attn_w].reshape(bp, seq, n_heads, hd))
        attn = _moba_prompt(hp, bias_diag, bias_adj, bias_far, batch=bp, seq=seq, n_heads=n_heads)
        ret, s_new = _ret_prompt(hp, cos_p, sin_p, tabs, s0_prompt, batch=bp, seq=seq, n_heads=n_heads, col0=3)
        outs["rp"].append(s_new)
        x1_f, x1_b = _outproj_ln(attn, ret, w_out_b, l, xp_f, ln1_g3, ln1_b3, alpha=alpha, tm=512)
        gact, conv_new = _ffn_up_prompt(x1_b, w_gate_b, w_up_b, conv_w, conv_b3, conv0_prompt, l,
                                        batch=bp, seq=seq, tm=1024, tn=512)
        outs["cp"].append(conv_new)
        x2 = _ffn_down_ln(gact, w_down_b, l, x1_f, ln2_g3, ln2_b3, alpha=alpha, tm=512, tk=nf // 4)
        xp_f, xp_b = _ple(x2, p_prompt[l].reshape(mp, -1), w_pg_b, w_ple_b, l, tm=512)

        hs = _matmul(xs_b, w_in_b, l, tm=bs, tn=1024)
        outs["ks"].append(hs[:, attn_w:2 * attn_w].reshape(bs, 1, n_heads, hd))
        outs["vs"].append(hs[:, 2 * attn_w:3 * attn_w].reshape(bs, 1, n_heads, hd))
        hs3 = hs.reshape(bs, 1, -1)
        picks = _sample_select(hs3, ksum[l], n_heads=n_heads)
        blocks = jnp.transpose(picks[:, :MOBA_TOPK, :n_heads], (0, 2, 1))
        sel_pages = blocks[..., None] * ppb + jnp.arange(ppb, dtype=jnp.int32)
        sel_ids = page_ids[l][jnp.arange(bs)[:, None, None, None], sel_pages]
        near = jnp.broadcast_to((blocks == n_past_blocks - 1)[..., None], sel_pages.shape).astype(jnp.int32)
        attn_s = _sample_attn(hs3, ck, cv, sel_ids.reshape(-1), near.reshape(-1), bias_near_s, bias_far_s,
                              n_heads=n_heads, n_sel_pages=MOBA_TOPK * ppb)
        ret_s, s_new_s = _ret_sample(hs3, cos_s, sin_s, decay_s, state_ret[l], n_heads=n_heads, col0=3)
        outs["rs"].append(s_new_s)
        x1s_f, x1s_b = _outproj_ln(attn_s.reshape(bs, attn_w), ret_s.reshape(bs, attn_w), w_out_b, l, xs_f,
                                   ln1_g3, ln1_b3, alpha=alpha, tm=bs)
        gact_s, c0, c1 = _ffn_up_sample(x1s_b, w_gate_b, w_up_b, conv_w, conv_b3,
                                        state_conv[l, :, 0], state_conv[l, :, 1], l, tn=512)
        outs["cs"].append(jnp.stack([c0, c1], axis=1))
        x2s = _ffn_down_ln(gact_s, w_down_b, l, x1s_f, ln2_g3, ln2_b3, alpha=alpha, tm=bs, tk=nf // 4)
        xs_f, xs_b = _ple(x2s, p_sample[l].reshape(bs, -1), w_pg_b, w_ple_b, l, tm=bs)

    stack = lambda key: jnp.stack(outs[key])
    return (xp_f.reshape(bp, seq, d), xs_f.reshape(bs, 1, d),
            stack("kp"), stack("vp"), stack("ks"), stack("vs"),
            stack("rp"), stack("rs"), stack("cp"), stack("cs"))
```

```python
import functools
import math

import jax
import jax.numpy as jnp
import numpy as np
from jax import lax
from jax.experimental import pallas as pl
from jax.experimental.pallas import tpu as pltpu

HEAD_DIM = 128
MOBA_BLOCK = 256
MOBA_TOPK = 3
RET_CHUNK = 128
ROPE_BASE = 10000.0
NUM_BUCKETS = 32
MAX_DISTANCE = 128
CONV_W = 3
LN_EPS = 1e-5

VMEM_LIMIT_BYTES = 56 * 1024 * 1024
NEG_BIG = -1e30

BF16 = jnp.bfloat16
F32 = jnp.float32

_NT = (((1,), (1,)), ((), ()))
_TN = (((0,), (0,)), ((), ()))


def _cparams(*sem):
    return pltpu.CompilerParams(dimension_semantics=sem, vmem_limit_bytes=VMEM_LIMIT_BYTES)


def _sigmoid(x):
    return 1.0 / (1.0 + jnp.exp(-x))


def _layer_norm_rows(y, g, b):
    mu = jnp.mean(y, axis=-1, keepdims=True)
    d = y - mu
    var = jnp.mean(d * d, axis=-1, keepdims=True)
    return d * lax.rsqrt(var + LN_EPS) * g + b


def _split_bf16(a):
    hi = a.astype(BF16)
    lo = (a - hi.astype(F32)).astype(BF16)
    return hi, lo


def _dot_nt_precise(a, b):
    ah, al = _split_bf16(a)
    bh, bl = _split_bf16(b)
    dot = functools.partial(lax.dot_general, dimension_numbers=_NT, preferred_element_type=F32)
    return dot(ah, bh) + (dot(ah, bl) + dot(al, bh))


def _mm_kernel(x_ref, w_ref, o_ref, wb_ref):
    @pl.when(pl.program_id(1) == 0)
    def _():
        wb_ref[...] = w_ref[0].astype(BF16)

    o_ref[...] = jnp.dot(x_ref[...], wb_ref[...], preferred_element_type=F32).astype(o_ref.dtype)


def _matmul(x, w, layer, *, col0, n, tm, tn, out_dtype=F32):
    m, k = x.shape
    assert col0 % tn == 0 and n % tn == 0
    return pl.pallas_call(
        _mm_kernel,
        grid=(n // tn, m // tm),
        in_specs=[
            pl.BlockSpec((tm, k), lambda j, i: (i, 0)),
            pl.BlockSpec((1, k, tn), lambda j, i: (layer, 0, col0 // tn + j)),
        ],
        out_specs=pl.BlockSpec((tm, tn), lambda j, i: (i, j)),
        out_shape=jax.ShapeDtypeStruct((m, n), out_dtype),
        scratch_shapes=[pltpu.VMEM((k, tn), BF16)],
        compiler_params=_cparams("parallel", "arbitrary"),
        name="proj_in",
    )(x, w)


def _moba_select(cc, qf, km_ref):
    tq = qf.shape[0]
    nbp = km_ref.shape[0]
    gate = _dot_nt_precise(km_ref[...], qf)
    blk = lax.broadcasted_iota(jnp.int32, (nbp, tq), 0)
    eligible = blk < cc
    gate = jnp.where(eligible, gate, -jnp.inf)
    rank = jnp.zeros((nbp, tq), F32)
    for mth in range(cc):
        gm = gate[mth:mth + 1, :]
        beats = jnp.logical_or(gm > gate, jnp.logical_and(gm == gate, mth < blk))
        rank = rank + jnp.where(beats, 1.0, 0.0)
    sel_t = jnp.where(jnp.logical_and(eligible, rank < MOBA_TOPK), 1.0, 0.0)
    sel_pad = jnp.concatenate([sel_t, jnp.zeros((128 - nbp, tq), F32)], axis=0).astype(BF16)
    rows = lax.broadcasted_iota(jnp.int32, (tq, tq), 0)
    cols = lax.broadcasted_iota(jnp.int32, (tq, tq), 1)
    eye = jnp.where(rows == cols, 1.0, 0.0).astype(BF16)
    return lax.dot_general(eye, sel_pad, _NT, preferred_element_type=F32)


def _moba_tile(cc, q_ref, bd_ref, ba_ref, bf_ref, o_ref, kb_ref, vb_ref, km_ref):
    tq = q_ref.shape[0]
    scale = HEAD_DIM ** -0.5
    nk = (cc + 1) * MOBA_BLOCK
    qf = q_ref[...]
    s = lax.dot_general(qf.astype(BF16), kb_ref[0:nk, :], _NT, preferred_element_type=F32) * scale
    sel = _moba_select(cc, qf, km_ref) if cc > MOBA_TOPK else None
    rows = lax.broadcasted_iota(jnp.int32, (tq, MOBA_BLOCK), 0)
    cols = lax.broadcasted_iota(jnp.int32, (tq, MOBA_BLOCK), 1)
    far_bias = bf_ref[0, 0:1, 0:1]
    pieces = []
    for n in range(cc + 1):
        sn = s[:, n * MOBA_BLOCK:(n + 1) * MOBA_BLOCK]
        if n == cc:
            sn = jnp.where(cols <= rows, sn + bd_ref[0], NEG_BIG)
        elif n == cc - 1:
            sn = sn + ba_ref[0]
        else:
            sn = sn + far_bias
        if n < cc and sel is not None:
            sn = jnp.where(sel[:, n:n + 1] > 0.5, sn, NEG_BIG)
        pieces.append(sn)
    s = jnp.concatenate(pieces, axis=1) if cc else pieces[0]
    m = jnp.max(s, axis=-1, keepdims=True)
    p = jnp.exp(s - m)
    l = jnp.sum(p, axis=-1, keepdims=True)
    acc = jnp.dot(p.astype(BF16), vb_ref[0:nk, :], preferred_element_type=F32)
    o_ref[...] = (acc / l).astype(o_ref.dtype)


def _moba_prompt_kernel(q_ref, k_ref, v_ref, bd_ref, ba_ref, bf_ref, o_ref, kb_ref, vb_ref, km_ref):
    c = pl.program_id(2)
    nb = k_ref.shape[0] // MOBA_BLOCK

    @pl.when(c == 0)
    def _():
        kb_ref[...] = k_ref[...].astype(BF16)
        vb_ref[...] = v_ref[...].astype(BF16)
        km_ref[...] = jnp.zeros_like(km_ref)
        for n in range(nb):
            km_ref[n:n + 1, :] = jnp.mean(k_ref[n * MOBA_BLOCK:(n + 1) * MOBA_BLOCK, :], axis=0, keepdims=True)

    for cc in range(nb):
        pl.when(c == cc)(functools.partial(_moba_tile, cc, q_ref, bd_ref, ba_ref, bf_ref, o_ref,
                                           kb_ref, vb_ref, km_ref))


def _moba_prompt(q, k, v, bias_diag, bias_adj, bias_far, *, batch, seq, n_heads):
    nq = seq // MOBA_BLOCK
    nbias = bias_far.shape[1]
    return pl.pallas_call(
        _moba_prompt_kernel,
        grid=(batch, n_heads, nq),
        in_specs=[
            pl.BlockSpec((MOBA_BLOCK, HEAD_DIM), lambda b, hh, qi: (b * nq + qi, hh)),
            pl.BlockSpec((seq, HEAD_DIM), lambda b, hh, qi: (b, hh)),
            pl.BlockSpec((seq, HEAD_DIM), lambda b, hh, qi: (b, hh)),
            pl.BlockSpec((1, MOBA_BLOCK, MOBA_BLOCK), lambda b, hh, qi: (hh, 0, 0)),
            pl.BlockSpec((1, MOBA_BLOCK, MOBA_BLOCK), lambda b, hh, qi: (hh, 0, 0)),
            pl.BlockSpec((1, nbias, 128), lambda b, hh, qi: (hh, 0, 0)),
        ],
        out_specs=pl.BlockSpec((MOBA_BLOCK, HEAD_DIM), lambda b, hh, qi: (b * nq + qi, hh)),
        out_shape=jax.ShapeDtypeStruct((batch * seq, n_heads * HEAD_DIM), BF16),
        scratch_shapes=[
            pltpu.VMEM((seq, HEAD_DIM), BF16),
            pltpu.VMEM((seq, HEAD_DIM), BF16),
            pltpu.VMEM((max(16, seq // MOBA_BLOCK), HEAD_DIM), F32),
        ],
        compiler_params=_cparams("parallel", "parallel", "arbitrary"),
        name="moba_prompt",
    )(q, k, v, bias_diag, bias_adj, bias_far)


def _rotate(x, cos, sin_signed):
    return x * cos + pltpu.roll(x, HEAD_DIM // 2, 1) * sin_signed


def _head_norm_gate(o, g):
    mu = jnp.mean(o, axis=-1, keepdims=True)
    d = o - mu
    var = jnp.mean(d * d, axis=-1, keepdims=True)
    return g * _sigmoid(g) * (d * lax.rsqrt(var + LN_EPS))


def _ret_prompt_kernel(q_ref, k_ref, v_ref, g_ref, cos_ref, sin_ref, dm_ref, cr_ref, up_ref, cd_ref,
                       s0_ref, o_ref, s_ref):
    n_heads = s_ref.shape[1]
    scale = HEAD_DIM ** -0.5

    @pl.when(pl.program_id(1) == 0)
    def _():
        s_ref[...] = s0_ref[...]

    cos = cos_ref[...]
    sin = sin_ref[...]
    for hh in range(n_heads):
        sl = slice(hh * HEAD_DIM, (hh + 1) * HEAD_DIM)
        q = _rotate(q_ref[:, sl], cos, sin)
        k = _rotate(k_ref[:, sl], cos, sin) * scale
        qb = q.astype(BF16)
        kb = k.astype(BF16)
        vb = v_ref[:, sl].astype(BF16)
        s = s_ref[0, hh]
        sc = lax.dot_general(qb, kb, _NT, preferred_element_type=F32) * dm_ref[hh]
        o = jnp.dot(sc.astype(BF16), vb, preferred_element_type=F32)
        o = o + jnp.dot(qb, s.astype(BF16), preferred_element_type=F32) * cr_ref[hh]
        ku = (k * up_ref[hh]).astype(BF16)
        s_ref[0, hh] = s * cd_ref[hh] + lax.dot_general(ku, vb, _TN, preferred_element_type=F32)
        o_ref[:, sl] = _head_norm_gate(o, g_ref[:, sl]).astype(o_ref.dtype)


def _ret_prompt(h, cos, sin, tabs, s0, *, batch, seq, n_heads, col0):
    nc = seq // RET_CHUNK
    width = n_heads * HEAD_DIM
    hspec = lambda g: pl.BlockSpec((RET_CHUNK, width), lambda b, c: (b * nc + c, col0 + g))
    tspec = pl.BlockSpec((n_heads, RET_CHUNK, HEAD_DIM), lambda b, c: (0, 0, 0))
    sspec = pl.BlockSpec((1, n_heads, HEAD_DIM, HEAD_DIM), lambda b, c: (b, 0, 0, 0))
    return pl.pallas_call(
        _ret_prompt_kernel,
        grid=(batch, nc),
        in_specs=[hspec(0), hspec(1), hspec(2), hspec(3),
                  pl.BlockSpec((RET_CHUNK, HEAD_DIM), lambda b, c: (c, 0)),
                  pl.BlockSpec((RET_CHUNK, HEAD_DIM), lambda b, c: (c, 0)),
                  tspec, tspec, tspec, tspec, sspec],
        out_specs=[pl.BlockSpec((RET_CHUNK, width), lambda b, c: (b * nc + c, 0)), sspec],
        out_shape=[jax.ShapeDtypeStruct((batch * seq, width), BF16),
                   jax.ShapeDtypeStruct((batch, n_heads, HEAD_DIM, HEAD_DIM), F32)],
        compiler_params=_cparams("parallel", "arbitrary"),
        name="ret_prompt",
    )(h, h, h, h, cos, sin, *tabs, s0)


def _outproj_kernel(alpha, a_ref, r_ref, wa_ref, wr_ref, x_ref, g_ref, b_ref, of_ref, ob_ref):
    mix = jnp.dot(a_ref[...], wa_ref[0], preferred_element_type=F32)
    mix = mix + jnp.dot(r_ref[...], wr_ref[0], preferred_element_type=F32)
    y = _layer_norm_rows(alpha * x_ref[...] + mix, g_ref[0], b_ref[0])
    of_ref[...] = y
    ob_ref[...] = y.astype(BF16)


def _outproj_ln(attn, ret, w_out, layer, x, g, b, *, alpha, tm):
    m, ka = attn.shape
    d = x.shape[1]
    row = lambda width: pl.BlockSpec((tm, width), lambda i: (i, 0))
    vec = pl.BlockSpec((1, 1, d), lambda i: (layer, 0, 0))
    return pl.pallas_call(
        functools.partial(_outproj_kernel, alpha),
        grid=(m // tm,),
        in_specs=[row(ka), row(ka),
                  pl.BlockSpec((1, ka, d), lambda i: (layer, 0, 0)),
                  pl.BlockSpec((1, ka, d), lambda i: (layer, 1, 0)),
                  row(d), vec, vec],
        out_specs=[row(d), row(d)],
        out_shape=[jax.ShapeDtypeStruct((m, d), F32), jax.ShapeDtypeStruct((m, d), BF16)],
        compiler_params=_cparams("parallel"),
        name="outproj_ln",
    )(attn, ret, w_out, w_out, x, g, b)


def _ffn_up_prompt_kernel(tiles_per_seq, x_ref, wg_ref, wu_ref, cw_ref, cb_ref, buf_ref,
                          o_ref, cn_ref, carry_ref, wgb_ref, wub_ref):
    i = pl.program_id(1)
    tm = x_ref.shape[0]

    @pl.when(i == 0)
    def _():
        wgb_ref[...] = wg_ref[0].astype(BF16)
        wub_ref[...] = wu_ref[0].astype(BF16)

    x = x_ref[...]
    a = jnp.dot(x, wgb_ref[...], preferred_element_type=F32)
    u = jnp.dot(x, wub_ref[...], preferred_element_type=F32)
    first = (i % tiles_per_seq) == 0
    prev = jnp.where(first, buf_ref[0], carry_ref[6:8, :])
    row = lax.broadcasted_iota(jnp.int32, a.shape, 0)
    a1 = jnp.where(row == 0, prev[1:2, :], pltpu.roll(a, 1, 0))
    a2 = jnp.where(row == 0, prev[0:1, :], jnp.where(row == 1, prev[1:2, :], pltpu.roll(a, 2, 0)))
    cw = cw_ref[0]
    conv = cb_ref[0] + cw[0:1, :] * a2 + cw[1:2, :] * a1 + cw[2:3, :] * a
    o_ref[...] = (conv * _sigmoid(conv) * u).astype(o_ref.dtype)
    carry_ref[...] = a[tm - 8:tm, :]

    @pl.when((i % tiles_per_seq) == tiles_per_seq - 1)
    def _():
        cn_ref[0] = a[tm - (CONV_W - 1):tm, :]


def _ffn_up_prompt(x, w_gate, w_up, conv_w, conv_b, conv_buf, layer, *, batch, seq, tm, tn):
    m, d = x.shape
    nf = w_gate.shape[2]
    tps = seq // tm
    wspec = pl.BlockSpec((1, d, tn), lambda j, i: (layer, 0, j))
    return pl.pallas_call(
        functools.partial(_ffn_up_prompt_kernel, tps),
        grid=(nf // tn, m // tm),
        in_specs=[pl.BlockSpec((tm, d), lambda j, i: (i, 0)), wspec, wspec,
                  pl.BlockSpec((1, CONV_W, tn), lambda j, i: (layer, 0, j)),
                  pl.BlockSpec((1, 1, tn), lambda j, i: (layer, 0, j)),
                  pl.BlockSpec((1, CONV_W - 1, tn), lambda j, i: (i // tps, 0, j))],
        out_specs=[pl.BlockSpec((tm, tn), lambda j, i: (i, j)),
                   pl.BlockSpec((1, CONV_W - 1, tn), lambda j, i: (i // tps, 0, j))],
        out_shape=[jax.ShapeDtypeStruct((m, nf), BF16),
                   jax.ShapeDtypeStruct((batch, CONV_W - 1, nf), F32)],
        scratch_shapes=[pltpu.VMEM((8, tn), F32), pltpu.VMEM((d, tn), BF16), pltpu.VMEM((d, tn), BF16)],
        compiler_params=_cparams("parallel", "arbitrary"),
        name="ffn_up_prompt",
    )(x, w_gate, w_up, conv_w, conv_b, conv_buf)


def _ffn_up_sample_kernel(x_ref, wg_ref, wu_ref, cw_ref, cb_ref, b0_ref, b1_ref, o_ref, n0_ref, n1_ref):
    x = x_ref[...]
    a = jnp.dot(x, wg_ref[0].astype(BF16), preferred_element_type=F32)
    u = jnp.dot(x, wu_ref[0].astype(BF16), preferred_element_type=F32)
    cw = cw_ref[0]
    conv = cb_ref[0] + cw[0:1, :] * b0_ref[...] + cw[1:2, :] * b1_ref[...] + cw[2:3, :] * a
    o_ref[...] = (conv * _sigmoid(conv) * u).astype(o_ref.dtype)
    n0_ref[...] = b1_ref[...]
    n1_ref[...] = a


def _ffn_up_sample(x, w_gate, w_up, conv_w, conv_b, buf0, buf1, layer, *, tn):
    m, d = x.shape
    nf = w_gate.shape[2]
    wspec = pl.BlockSpec((1, d, tn), lambda j: (layer, 0, j))
    cspec = pl.BlockSpec((m, tn), lambda j: (0, j))
    return pl.pallas_call(
        _ffn_up_sample_kernel,
        grid=(nf // tn,),
        in_specs=[pl.BlockSpec((m, d), lambda j: (0, 0)), wspec, wspec,
                  pl.BlockSpec((1, CONV_W, tn), lambda j: (layer, 0, j)),
                  pl.BlockSpec((1, 1, tn), lambda j: (layer, 0, j)),
                  cspec, cspec],
        out_specs=[cspec, cspec, cspec],
        out_shape=[jax.ShapeDtypeStruct((m, nf), BF16),
                   jax.ShapeDtypeStruct((m, nf), F32),
                   jax.ShapeDtypeStruct((m, nf), F32)],
        compiler_params=_cparams("parallel"),
        name="ffn_up_sample",
    )(x, w_gate, w_up, conv_w, conv_b, buf0, buf1)


def _ffn_down_kernel(alpha, g_ref, w_ref, x_ref, lg_ref, lb_ref, o_ref, acc_ref):
    kk = pl.program_id(1)

    @pl.when(kk == 0)
    def _():
        acc_ref[...] = jnp.zeros_like(acc_ref)

    acc_ref[...] += jnp.dot(g_ref[...], w_ref[0], preferred_element_type=F32)

    @pl.when(kk == pl.num_programs(1) - 1)
    def _():
        o_ref[...] = _layer_norm_rows(alpha * x_ref[...] + acc_ref[...], lg_ref[0], lb_ref[0])


def _ffn_down_ln(g, w_down, layer, x, ln_g, ln_b, *, alpha, tm, tk):
    m, nf = g.shape
    d = x.shape[1]
    vec = pl.BlockSpec((1, 1, d), lambda i, kk: (layer, 0, 0))
    return pl.pallas_call(
        functools.partial(_ffn_down_kernel, alpha),
        grid=(m // tm, nf // tk),
        in_specs=[pl.BlockSpec((tm, tk), lambda i, kk: (i, kk)),
                  pl.BlockSpec((1, tk, d), lambda i, kk: (layer, kk, 0)),
                  pl.BlockSpec((tm, d), lambda i, kk: (i, 0)), vec, vec],
        out_specs=pl.BlockSpec((tm, d), lambda i, kk: (i, 0)),
        out_shape=jax.ShapeDtypeStruct((m, d), F32),
        scratch_shapes=[pltpu.VMEM((tm, d), F32)],
        compiler_params=_cparams("parallel", "arbitrary"),
        name="ffn_down_ln",
    )(g, w_down, x, ln_g, ln_b)


def _ple_kernel(x_ref, p_ref, wg_ref, wp_ref, of_ref, ob_ref):
    x = x_ref[...]
    gate = _sigmoid(jnp.dot(x.astype(BF16), wg_ref[0], preferred_element_type=F32))
    emb = jnp.dot(p_ref[...].astype(BF16), wp_ref[0], preferred_element_type=F32)
    y = x + gate * emb
    of_ref[...] = y
    ob_ref[...] = y.astype(BF16)


def _ple(x, p, w_ple_gate, w_ple, layer, *, tm):
    m, d = x.shape
    pd = p.shape[1]
    row = lambda width: pl.BlockSpec((tm, width), lambda i: (i, 0))
    return pl.pallas_call(
        _ple_kernel,
        grid=(m // tm,),
        in_specs=[row(d), row(pd),
                  pl.BlockSpec((1, d, d), lambda i: (layer, 0, 0)),
                  pl.BlockSpec((1, pd, d), lambda i: (layer, 0, 0))],
        out_specs=[row(d), row(d)],
        out_shape=[jax.ShapeDtypeStruct((m, d), F32), jax.ShapeDtypeStruct((m, d), BF16)],
        compiler_params=_cparams("parallel"),
        name="ple",
    )(x, p, w_ple_gate, w_ple)


KSUM_PAGES_PER_STEP = 8


def _block_ksum_kernel(ids_ref, *refs):
    del ids_ref
    page_refs, o_ref = refs[:-1], refs[-1]
    n_heads = o_ref.shape[2]
    ppb = len(page_refs) // o_ref.shape[1]
    for r, p_ref in enumerate(page_refs):
        x = p_ref[0, 0]
        part = jnp.sum(x.reshape(x.shape[0] // n_heads, n_heads, x.shape[1]), axis=0)
        if r % ppb == 0:
            o_ref[0, r // ppb] = part
        else:
            o_ref[0, r // ppb] += part


def _block_ksum(cache_k_rows, page_table_flat, pages_per_block, n_heads):
    depth, _, rows, hd = cache_k_rows.shape
    pps = KSUM_PAGES_PER_STEP
    n_pages = page_table_flat.shape[0]
    assert n_pages % pps == 0 and pps % pages_per_block == 0
    bps = pps // pages_per_block
    page_spec = lambda r: pl.BlockSpec((1, 1, rows, hd), lambda l, st, ids: (l, ids[pps * st + r], 0, 0))
    return pl.pallas_call(
        _block_ksum_kernel,
        grid_spec=pltpu.PrefetchScalarGridSpec(
            num_scalar_prefetch=1,
            grid=(depth, n_pages // pps),
            in_specs=[page_spec(r) for r in range(pps)],
            out_specs=pl.BlockSpec((1, bps, n_heads, hd), lambda l, st, ids: (l, st, 0, 0)),
        ),
        out_shape=jax.ShapeDtypeStruct((depth, n_pages // pages_per_block, n_heads, hd), F32),
        compiler_params=_cparams("parallel", "parallel"),
        name="block_ksum",
    )(page_table_flat, *([cache_k_rows] * pps))


def _sample_select_kernel(q_ref, ks_ref, o_ref):
    nblk, n_heads = ks_ref.shape[1], ks_ref.shape[2]
    prod = ks_ref[0] * q_ref[0][None] * (1.0 / MOBA_BLOCK)
    gate = jnp.sum(prod, axis=-1, keepdims=True)
    blk = lax.broadcasted_iota(jnp.int32, gate.shape, 0).astype(F32)
    for t in range(MOBA_TOPK):
        best = jnp.max(gate, axis=0, keepdims=True)
        idx = jnp.min(jnp.where(gate == best, blk, float(nblk)), axis=0, keepdims=True)
        o_ref[0, t] = jnp.broadcast_to(idx[0], (n_heads, 128)).astype(jnp.int32)
        gate = jnp.where(blk == idx, -jnp.inf, gate)


def _sample_select(q_heads, ksum_l):
    bs, nblk, n_heads, hd = ksum_l.shape
    return pl.pallas_call(
        _sample_select_kernel,
        grid=(bs,),
        in_specs=[pl.BlockSpec((1, n_heads, hd), lambda b: (b, 0, 0)),
                  pl.BlockSpec((1, nblk, n_heads, hd), lambda b: (b, 0, 0, 0))],
        out_specs=pl.BlockSpec((1, MOBA_TOPK, n_heads, 128), lambda b: (b, 0, 0, 0)),
        out_shape=jax.ShapeDtypeStruct((bs, MOBA_TOPK, n_heads, 128), jnp.int32),
        compiler_params=_cparams("parallel"),
        name="sample_select",
    )(q_heads, ksum_l)


def _sample_attn_kernel(n_sel_pages, pid_ref, near_ref, q_ref, kn_ref, vn_ref, *refs):
    del pid_ref
    kp_refs = refs[:n_sel_pages]
    vp_refs = refs[n_sel_pages:2 * n_sel_pages]
    bn_ref, bf_ref, o_ref = refs[2 * n_sel_pages:]
    b = pl.program_id(0)
    hh = pl.program_id(1)
    n_heads = pl.num_programs(1)
    page = bn_ref.shape[2]
    ppb = bn_ref.shape[1]
    stride = kp_refs[0].shape[2] // page
    scale = HEAD_DIM ** -0.5
    q = q_ref[0]
    qb = jnp.broadcast_to(q, (8, HEAD_DIM)).astype(BF16)
    ks = jnp.concatenate([r[0, 0, pl.ds(hh, page, stride=stride), :] for r in kp_refs], axis=0).astype(BF16)
    vs = jnp.concatenate([r[0, 0, pl.ds(hh, page, stride=stride), :] for r in vp_refs], axis=0).astype(BF16)
    s = lax.dot_general(qb, ks, _NT, preferred_element_type=F32)[0:1, :] * scale
    far_bias = bf_ref[0, 0:1, 0:1]
    bias = []
    for j in range(n_sel_pages):
        near = near_ref[(b * n_heads + hh) * (n_sel_pages // ppb) + j // ppb]
        bias.append(jnp.where(near == 1, bn_ref[0, j % ppb:j % ppb + 1, :], far_bias))
    s = s + jnp.concatenate(bias, axis=1)
    s_own = jnp.sum(q.astype(BF16).astype(F32) * kn_ref[0].astype(BF16).astype(F32), axis=-1, keepdims=True)
    s_own = s_own * scale + bf_ref[0, 1:2, 0:1]
    m = jnp.maximum(jnp.max(s, axis=-1, keepdims=True), s_own)
    p = jnp.exp(s - m)
    p_own = jnp.exp(s_own - m)
    l = jnp.sum(p, axis=-1, keepdims=True) + p_own
    pv = jnp.dot(jnp.broadcast_to(p, (8, p.shape[1])).astype(BF16), vs, preferred_element_type=F32)[0:1, :]
    acc = pv + p_own.astype(BF16).astype(F32) * vn_ref[0].astype(BF16).astype(F32)
    o_ref[0, 0] = (acc / l).astype(o_ref.dtype)


def _sample_attn(hs3, cache_k_rows, cache_v_rows, layer, sel_pages, near, bias_near, bias_far, *,
                 n_heads, n_sel_pages):
    bs = hs3.shape[0]
    _, _, rows, hd = cache_k_rows.shape
    page_spec = lambda j: pl.BlockSpec(
        (1, 1, rows, hd), lambda b, hh, pid, nr: (layer, pid[(b * n_heads + hh) * n_sel_pages + j], 0, 0))
    hspec = lambda g: pl.BlockSpec((1, 1, HEAD_DIM), lambda b, hh, pid, nr: (b, 0, g * n_heads + hh))
    return pl.pallas_call(
        functools.partial(_sample_attn_kernel, n_sel_pages),
        grid_spec=pltpu.PrefetchScalarGridSpec(
            num_scalar_prefetch=2,
            grid=(bs, n_heads),
            in_specs=[hspec(0), hspec(1), hspec(2)]
            + [page_spec(j) for j in range(n_sel_pages)] * 2
            + [pl.BlockSpec((1,) + bias_near.shape[1:], lambda b, hh, pid, nr: (hh, 0, 0)),
               pl.BlockSpec((1, 2, 128), lambda b, hh, pid, nr: (hh, 0, 0))],
            out_specs=pl.BlockSpec((1, 1, 1, HEAD_DIM), lambda b, hh, pid, nr: (b, hh, 0, 0)),
        ),
        out_shape=jax.ShapeDtypeStruct((bs, n_heads, 1, HEAD_DIM), BF16),
        compiler_params=_cparams("parallel", "arbitrary"),
        name="sample_attn",
    )(sel_pages, near, hs3, hs3, hs3, *([cache_k_rows] * n_sel_pages), *([cache_v_rows] * n_sel_pages),
      bias_near, bias_far)


def _ret_sample_kernel(q_ref, k_ref, v_ref, g_ref, cos_ref, sin_ref, dec_ref, s0_ref, o_ref, s_ref):
    n_heads = s_ref.shape[1]
    scale = HEAD_DIM ** -0.5
    cos = cos_ref[...]
    sin = sin_ref[...]
    rows = lax.broadcasted_iota(jnp.int32, (HEAD_DIM, HEAD_DIM), 0)
    cols = lax.broadcasted_iota(jnp.int32, (HEAD_DIM, HEAD_DIM), 1)
    eye = jnp.where(rows == cols, 1.0, 0.0)
    for hh in range(n_heads):
        sl = slice(hh * HEAD_DIM, (hh + 1) * HEAD_DIM)
        q = _rotate(q_ref[0, :, sl], cos, sin)
        k = _rotate(k_ref[0, :, sl], cos, sin) * scale
        v = v_ref[0, :, sl]
        g = g_ref[0, :, sl]
        dec = dec_ref[hh:hh + 1, :]
        q_col = jnp.sum(eye * q, axis=-1, keepdims=True)
        k_col = jnp.sum(eye * k, axis=-1, keepdims=True)
        s = s0_ref[0, hh]
        qk = jnp.sum(q * k, axis=-1, keepdims=True)
        o = qk * v + jnp.sum(q_col * s, axis=0, keepdims=True) * dec
        s_ref[0, hh] = s * dec + k_col * v
        o_ref[0, :, sl] = _head_norm_gate(o, g).astype(o_ref.dtype)


def _ret_sample(hs, cos, sin, decay, s0, *, n_heads, col0):
    bs = hs.shape[0]
    width = n_heads * HEAD_DIM
    hspec = lambda g: pl.BlockSpec((1, 1, width), lambda b: (b, 0, col0 + g))
    vspec = pl.BlockSpec((1, HEAD_DIM), lambda b: (0, 0))
    sspec = pl.BlockSpec((1, n_heads, HEAD_DIM, HEAD_DIM), lambda b: (b, 0, 0, 0))
    return pl.pallas_call(
        _ret_sample_kernel,
        grid=(bs,),
        in_specs=[hspec(0), hspec(1), hspec(2), hspec(3), vspec, vspec,
                  pl.BlockSpec((n_heads, HEAD_DIM), lambda b: (0, 0)), sspec],
        out_specs=[pl.BlockSpec((1, 1, width), lambda b: (b, 0, 0)), sspec],
        out_shape=[jax.ShapeDtypeStruct((bs, 1, width), BF16),
                   jax.ShapeDtypeStruct((bs, n_heads, HEAD_DIM, HEAD_DIM), F32)],
        compiler_params=_cparams("parallel"),
        name="ret_sample",
    )(hs, hs, hs, hs, cos, sin, decay, s0)


def _t5_bucket(rel):
    n = jnp.maximum(rel, 0)
    max_exact = NUM_BUCKETS // 2
    nf = jnp.maximum(n, max_exact).astype(F32)
    large = max_exact + (jnp.log(nf / max_exact) / math.log(MAX_DISTANCE / max_exact)
                         * (NUM_BUCKETS - max_exact)).astype(jnp.int32)
    return jnp.where(n < max_exact, n, jnp.minimum(large, NUM_BUCKETS - 1))


def _rope_tables(pos):
    half = HEAD_DIM // 2
    inv = ROPE_BASE ** (-jnp.arange(half, dtype=F32) / half)
    ang = pos[:, None] * inv[None, :]
    cos = jnp.cos(ang)
    sin = jnp.sin(ang)
    return jnp.concatenate([cos, cos], -1), jnp.concatenate([-sin, sin], -1)


def _decay_tables(n_heads, chunk):
    lg = jnp.log(1.0 - 2.0 ** (-5.0 - jnp.arange(n_heads, dtype=F32)))
    i = jnp.arange(chunk, dtype=F32)
    diff = i[:, None] - i[None, :]
    dmask = jnp.where(diff >= 0, jnp.exp(lg[:, None, None] * jnp.maximum(diff, 0.0)), 0.0)
    full = lambda col: jnp.broadcast_to(col[..., None], (n_heads, chunk, HEAD_DIM))
    cross = full(jnp.exp(lg[:, None] * (i + 1.0)))
    upd = full(jnp.exp(lg[:, None] * (chunk - 1.0 - i)))
    cdec = jnp.broadcast_to(jnp.exp(lg * chunk)[:, None, None], (n_heads, chunk, HEAD_DIM))
    return dmask, cross, upd, cdec


def kernel(x_prompt, x_sample, cache_k, cache_v, state_ret, state_conv, page_table, p_prompt, p_sample,
           rel_bias, w_in, w_out, ln1_g, ln1_b, w_gate, w_up, conv_w, conv_b, w_down, ln2_g, ln2_b,
           w_ple, w_ple_gate):
    bp, seq, d = x_prompt.shape
    bs, dec_seq, _ = x_sample.shape
    depth, n_pool, page, n_heads, hd = cache_k.shape
    n_pages = page_table.shape[1]
    past_len = n_pages * page
    nf = w_gate.shape[2]
    attn_w = n_heads * hd
    assert hd == HEAD_DIM and dec_seq == 1 and MOBA_BLOCK % page == 0
    assert seq % MOBA_BLOCK == 0 and past_len % MOBA_BLOCK == 0 and past_len // MOBA_BLOCK >= MOBA_TOPK
    assert w_in.shape[2] == 7 * attn_w and d == 2 * attn_w and RET_CHUNK == HEAD_DIM
    assert int(np.floor(np.log(np.float32(MOBA_BLOCK + 1) / 16) / math.log(MAX_DISTANCE / 16) * 16)) >= 15
    alpha = (2 * depth) ** 0.25
    ppb = MOBA_BLOCK // page
    n_past_blocks = past_len // MOBA_BLOCK
    mp = bp * seq

    w_out_b, w_down_b, w_ple_b, w_pg_b = (w.astype(BF16) for w in (w_out, w_down, w_ple, w_ple_gate))
    vec3 = lambda a: a.reshape(depth, 1, a.shape[-1])
    ln1_g3, ln1_b3, ln2_g3, ln2_b3, conv_b3 = map(vec3, (ln1_g, ln1_b, ln2_g, ln2_b, conv_b))

    def bias_of(rel):
        onehot = (_t5_bucket(rel)[..., None] == jnp.arange(NUM_BUCKETS)).astype(F32)
        return jnp.einsum("...b,bh->h...", onehot, rel_bias.astype(F32), precision=lax.Precision.HIGHEST)

    ii = jnp.arange(MOBA_BLOCK, dtype=jnp.int32)
    bias_diag = bias_of(ii[:, None] - ii[None, :])
    bias_adj = bias_of(MOBA_BLOCK + ii[:, None] - ii[None, :])
    far = rel_bias[NUM_BUCKETS - 1].astype(F32)
    bias_far = jnp.broadcast_to(far[:, None, None], (n_heads, 1, 128))
    bias_near_s = bias_of(MOBA_BLOCK - ii).reshape(n_heads, ppb, page)
    bias_far_s = jnp.stack([jnp.broadcast_to(far[:, None], (n_heads, 128)),
                            jnp.broadcast_to(rel_bias[0].astype(F32)[:, None], (n_heads, 128))], axis=1)

    cos_p, sin_p = _rope_tables(jnp.arange(seq, dtype=F32))
    cos_s, sin_s = _rope_tables(jnp.full((1,), float(past_len), F32))
    tabs = _decay_tables(n_heads, RET_CHUNK)
    decay_s = jnp.broadcast_to(
        (1.0 - 2.0 ** (-5.0 - jnp.arange(n_heads, dtype=F32)))[:, None], (n_heads, HEAD_DIM))
    decay_s = jnp.exp(jnp.log(decay_s))

    ck = cache_k.reshape(depth, n_pool, page * n_heads, hd)
    cv = cache_v.reshape(depth, n_pool, page * n_heads, hd)
    ksum = _block_ksum(ck, page_table.reshape(-1), ppb, n_heads)
    ksum = ksum.reshape(depth, bs, n_past_blocks, n_heads, hd)

    xp_f = x_prompt.reshape(mp, d)
    xp_b = xp_f.astype(BF16)
    xs_f = x_sample.reshape(bs, d)
    xs_b = xs_f.astype(BF16)
    s0_prompt = jnp.zeros((bp, n_heads, hd, hd), F32)
    conv0_prompt = jnp.zeros((bp, CONV_W - 1, nf), F32)

    outs = {k: [] for k in ("kp", "vp", "ks", "vs", "rp", "rs", "cp", "cs")}
    for l in range(depth):
        proj = functools.partial(_matmul, xp_b, w_in, l, tm=1024, tn=1024)
        qp = proj(col0=0, n=attn_w)
        kp = proj(col0=attn_w, n=attn_w)
        vp = proj(col0=2 * attn_w, n=attn_w)
        hr = proj(col0=3 * attn_w, n=4 * attn_w)
        outs["kp"].append(kp.reshape(bp, seq, n_heads, hd))
        outs["vp"].append(vp.reshape(bp, seq, n_heads, hd))
        attn = _moba_prompt(qp, kp, vp, bias_diag, bias_adj, bias_far, batch=bp, seq=seq, n_heads=n_heads)
        ret, s_new = _ret_prompt(hr, cos_p, sin_p, tabs, s0_prompt, batch=bp, seq=seq, n_heads=n_heads, col0=0)
        outs["rp"].append(s_new)
        x1_f, x1_b = _outproj_ln(attn, ret, w_out_b, l, xp_f, ln1_g3, ln1_b3, alpha=alpha, tm=512)
        gact, conv_new = _ffn_up_prompt(x1_b, w_gate, w_up, conv_w, conv_b3, conv0_prompt, l,
                                        batch=bp, seq=seq, tm=1024, tn=512)
        outs["cp"].append(conv_new)
        x2 = _ffn_down_ln(gact, w_down_b, l, x1_f, ln2_g3, ln2_b3, alpha=alpha, tm=512, tk=nf // 4)
        xp_f, xp_b = _ple(x2, p_prompt[l].reshape(mp, -1), w_pg_b, w_ple_b, l, tm=512)

        hs = _matmul(xs_b, w_in, l, col0=0, n=w_in.shape[2], tm=bs, tn=1024)
        outs["ks"].append(hs[:, attn_w:2 * attn_w].reshape(bs, 1, n_heads, hd))
        outs["vs"].append(hs[:, 2 * attn_w:3 * attn_w].reshape(bs, 1, n_heads, hd))
        hs3 = hs.reshape(bs, 1, -1)
        picks = _sample_select(hs[:, :attn_w].reshape(bs, n_heads, hd), ksum[l])
        blocks = jnp.transpose(picks[..., 0], (0, 2, 1))
        sel_pos = blocks[..., None] * ppb + jnp.arange(ppb, dtype=jnp.int32)
        sel_pages = page_table[jnp.arange(bs)[:, None, None, None], sel_pos]
        near = (blocks == n_past_blocks - 1).astype(jnp.int32)
        attn_s = _sample_attn(hs3, ck, cv, l, sel_pages.reshape(-1), near.reshape(-1), bias_near_s, bias_far_s,
                              n_heads=n_heads, n_sel_pages=MOBA_TOPK * ppb)
        ret_s, s_new_s = _ret_sample(hs3, cos_s, sin_s, decay_s, state_ret[l], n_heads=n_heads, col0=3)
        outs["rs"].append(s_new_s)
        x1s_f, x1s_b = _outproj_ln(attn_s.reshape(bs, attn_w), ret_s.reshape(bs, attn_w), w_out_b, l, xs_f,
                                   ln1_g3, ln1_b3, alpha=alpha, tm=bs)
        gact_s, c0, c1 = _ffn_up_sample(x1s_b, w_gate, w_up, conv_w, conv_b3,
                                        state_conv[l, :, 0], state_conv[l, :, 1], l, tn=512)
        outs["cs"].append(jnp.stack([c0, c1], axis=1))
        x2s = _ffn_down_ln(gact_s, w_down_b, l, x1s_f, ln2_g3, ln2_b3, alpha=alpha, tm=bs, tk=nf // 4)
        xs_f, xs_b = _ple(x2s, p_sample[l].reshape(bs, -1), w_pg_b, w_ple_b, l, tm=bs)

    stack = lambda key: jnp.stack(outs[key])
    return (xp_f.reshape(bp, seq, d), xs_f.reshape(bs, 1, d),
            stack("kp"), stack("vp"), stack("ks"), stack("vs"),
            stack("rp"), stack("rs"), stack("cp"), stack("cs"))
```

```python
import functools
import math

import jax
import jax.numpy as jnp
import numpy as np
from jax import lax
from jax.experimental import pallas as pl
from jax.experimental.pallas import tpu as pltpu

HEAD_DIM = 128
MOBA_BLOCK = 256
MOBA_TOPK = 3
RET_CHUNK = 128
ROPE_BASE = 10000.0
NUM_BUCKETS = 32
MAX_DISTANCE = 128
CONV_W = 3
LN_EPS = 1e-5

VMEM_LIMIT_BYTES = 56 * 1024 * 1024
NEG_BIG = -1e30
LOG2E = math.log2(math.e)

BF16 = jnp.bfloat16
F32 = jnp.float32

_NT = (((1,), (1,)), ((), ()))
_TN = (((0,), (0,)), ((), ()))


def _cparams(*sem):
    return pltpu.CompilerParams(dimension_semantics=sem, vmem_limit_bytes=VMEM_LIMIT_BYTES)


def _sigmoid(x):
    return 1.0 / (1.0 + jnp.exp(-x))


def _layer_norm_rows(y, g, b):
    mu = jnp.mean(y, axis=-1, keepdims=True)
    d = y - mu
    var = jnp.mean(d * d, axis=-1, keepdims=True)
    return d * lax.rsqrt(var + LN_EPS) * g + b


def _split_bf16(a):
    hi = a.astype(BF16)
    lo = (a - hi.astype(F32)).astype(BF16)
    return hi, lo


def _dot_nt_precise(a, b):
    ah, al = _split_bf16(a)
    bh, bl = _split_bf16(b)
    dot = functools.partial(lax.dot_general, dimension_numbers=_NT, preferred_element_type=F32)
    return dot(ah, bh) + (dot(ah, bl) + dot(al, bh))


def _mm_kernel(x_ref, w_ref, o_ref, wb_ref):
    @pl.when(pl.program_id(1) == 0)
    def _():
        wb_ref[...] = w_ref[0].astype(BF16)

    o_ref[...] = jnp.dot(x_ref[...], wb_ref[...], preferred_element_type=F32).astype(o_ref.dtype)


def _matmul(x, w, layer, *, col0, n, tm, tn, out_dtype=F32):
    m, k = x.shape
    assert col0 % tn == 0 and n % tn == 0
    return pl.pallas_call(
        _mm_kernel,
        grid=(n // tn, m // tm),
        in_specs=[
            pl.BlockSpec((tm, k), lambda j, i: (i, 0)),
            pl.BlockSpec((1, k, tn), lambda j, i: (layer, 0, col0 // tn + j)),
        ],
        out_specs=pl.BlockSpec((tm, tn), lambda j, i: (i, j)),
        out_shape=jax.ShapeDtypeStruct((m, n), out_dtype),
        scratch_shapes=[pltpu.VMEM((k, tn), BF16)],
        compiler_params=_cparams("parallel", "arbitrary"),
        name="proj_in",
    )(x, w)


def _mm_stacked_kernel(slab, x_ref, w_ref, *rest):
    o_ref, wb_ref = rest[-2:]

    @pl.when(pl.program_id(1) == 0)
    def _():
        wb_ref[...] = w_ref[0].astype(BF16)

    for other in range(o_ref.shape[0]):
        if other != slab:
            o_ref[other] = jnp.zeros(o_ref.shape[1:], o_ref.dtype)
    o_ref[slab] = jnp.dot(x_ref[...], wb_ref[...], preferred_element_type=F32)


def _matmul_stacked(x, w, layer, stack, *, col0, n, tm, tn):
    m, k = x.shape
    depth = w.shape[0]
    assert col0 % tn == 0 and n % tn == 0
    in_specs = [pl.BlockSpec((tm, k), lambda j, i: (i, 0)),
                pl.BlockSpec((1, k, tn), lambda j, i: (layer, 0, col0 // tn + j))]
    if stack is None:
        operands, aliases, slab = [x, w], {}, layer
        out_spec = pl.BlockSpec((depth, tm, tn), lambda j, i: (0, i, j))
    else:
        in_specs.append(pl.BlockSpec(memory_space=pl.ANY))
        operands, aliases, slab = [x, w, stack], {2: 0}, 0
        out_spec = pl.BlockSpec((1, tm, tn), lambda j, i: (layer, i, j))
    return pl.pallas_call(
        functools.partial(_mm_stacked_kernel, slab),
        grid=(n // tn, m // tm),
        in_specs=in_specs,
        out_specs=out_spec,
        out_shape=jax.ShapeDtypeStruct((depth, m, n), F32),
        scratch_shapes=[pltpu.VMEM((k, tn), BF16)],
        input_output_aliases=aliases,
        compiler_params=_cparams("parallel", "arbitrary"),
        name="proj_kv",
    )(*operands)


def _moba_select(cc, qf, km_ref):
    tq = qf.shape[0]
    nbp = km_ref.shape[0]
    gate = _dot_nt_precise(km_ref[...], qf)
    blk = lax.broadcasted_iota(jnp.int32, (nbp, tq), 0)
    eligible = blk < cc
    gate = jnp.where(eligible, gate, -jnp.inf)
    rank = jnp.zeros((nbp, tq), F32)
    for mth in range(cc):
        gm = gate[mth:mth + 1, :]
        beats = jnp.logical_or(gm > gate, jnp.logical_and(gm == gate, mth < blk))
        rank = rank + jnp.where(beats, 1.0, 0.0)
    sel_t = jnp.where(jnp.logical_and(eligible, rank < MOBA_TOPK), 1.0, 0.0)
    sel_pad = jnp.concatenate([sel_t, jnp.zeros((128 - nbp, tq), F32)], axis=0).astype(BF16)
    rows = lax.broadcasted_iota(jnp.int32, (tq, tq), 0)
    cols = lax.broadcasted_iota(jnp.int32, (tq, tq), 1)
    eye = jnp.where(rows == cols, 1.0, 0.0).astype(BF16)
    return lax.dot_general(eye, sel_pad, _NT, preferred_element_type=F32)


def _moba_tile(cc, q_ref, bd_ref, ba_ref, bf_ref, o_ref, kb_ref, vb_ref, km_ref):
    tq = q_ref.shape[0]
    scale2 = HEAD_DIM ** -0.5 * LOG2E
    nk = (cc + 1) * MOBA_BLOCK
    rows = lax.broadcasted_iota(jnp.int32, (tq, MOBA_BLOCK), 0)
    cols = lax.broadcasted_iota(jnp.int32, (tq, MOBA_BLOCK), 1)
    for hp in range(q_ref.shape[1] // HEAD_DIM):
        sl = slice(hp * HEAD_DIM, (hp + 1) * HEAD_DIM)
        qf = q_ref[:, sl]
        s = lax.dot_general(qf.astype(BF16), kb_ref[0:nk, sl], _NT, preferred_element_type=F32) * scale2
        sel = _moba_select(cc, qf, km_ref.at[hp]) if cc > MOBA_TOPK else None
        far_bias = bf_ref[hp, 0:1, 0:1]
        pieces = []
        for n in range(cc + 1):
            sn = s[:, n * MOBA_BLOCK:(n + 1) * MOBA_BLOCK]
            if n == cc:
                sn = jnp.where(cols <= rows, sn + bd_ref[hp], NEG_BIG)
            elif n == cc - 1:
                sn = sn + ba_ref[hp]
            else:
                sn = sn + far_bias
            if n < cc and sel is not None:
                sn = jnp.where(sel[:, n:n + 1] > 0.5, sn, NEG_BIG)
            pieces.append(sn)
        s = jnp.concatenate(pieces, axis=1) if cc else pieces[0]
        m = jnp.max(s, axis=-1, keepdims=True)
        p = jnp.exp2(s - m)
        l = jnp.sum(p, axis=-1, keepdims=True)
        acc = jnp.dot(p.astype(BF16), vb_ref[0:nk, sl], preferred_element_type=F32)
        o_ref[:, sl] = (acc / l).astype(o_ref.dtype)


def _moba_prompt_kernel(q_ref, k_ref, v_ref, bd_ref, ba_ref, bf_ref, o_ref, kb_ref, vb_ref, km_ref):
    c = pl.program_id(2)
    nb = k_ref.shape[1] // MOBA_BLOCK

    @pl.when(c == 0)
    def _():
        kb_ref[...] = k_ref[0].astype(BF16)
        vb_ref[...] = v_ref[0].astype(BF16)
        km_ref[...] = jnp.zeros_like(km_ref)
        for hp in range(km_ref.shape[0]):
            sl = slice(hp * HEAD_DIM, (hp + 1) * HEAD_DIM)
            for n in range(nb):
                km_ref[hp, n:n + 1, :] = jnp.mean(k_ref[0, n * MOBA_BLOCK:(n + 1) * MOBA_BLOCK, sl],
                                                  axis=0, keepdims=True)

    for cc in range(nb):
        pl.when(c == cc)(functools.partial(_moba_tile, cc, q_ref, bd_ref, ba_ref, bf_ref, o_ref,
                                           kb_ref, vb_ref, km_ref))


MOBA_HEADS_PER_STEP = 2


def _moba_prompt(q, k, v, layer, bias_diag, bias_adj, bias_far, *, batch, seq, n_heads):
    nq = seq // MOBA_BLOCK
    nbias = bias_far.shape[1]
    hps = MOBA_HEADS_PER_STEP
    width = hps * HEAD_DIM
    kvspec = pl.BlockSpec((1, seq, width), lambda b, hh, qi: (layer, b, hh))
    bspec = pl.BlockSpec((hps, MOBA_BLOCK, MOBA_BLOCK), lambda b, hh, qi: (hh, 0, 0))
    return pl.pallas_call(
        _moba_prompt_kernel,
        grid=(batch, n_heads // hps, nq),
        in_specs=[
            pl.BlockSpec((MOBA_BLOCK, width), lambda b, hh, qi: (b * nq + qi, hh)),
            kvspec, kvspec, bspec, bspec,
            pl.BlockSpec((hps, nbias, 128), lambda b, hh, qi: (hh, 0, 0)),
        ],
        out_specs=pl.BlockSpec((MOBA_BLOCK, width), lambda b, hh, qi: (b * nq + qi, hh)),
        out_shape=jax.ShapeDtypeStruct((batch * seq, n_heads * HEAD_DIM), BF16),
        scratch_shapes=[
            pltpu.VMEM((seq, width), BF16),
            pltpu.VMEM((seq, width), BF16),
            pltpu.VMEM((hps, max(16, seq // MOBA_BLOCK), HEAD_DIM), F32),
        ],
        compiler_params=_cparams("parallel", "parallel", "arbitrary"),
        name="moba_prompt",
    )(q, k, v, bias_diag, bias_adj, bias_far)


def _rotate(x, cos, sin_signed):
    return x * cos + pltpu.roll(x, HEAD_DIM // 2, 1) * sin_signed


def _head_norm_gate(o, g):
    mu = jnp.mean(o, axis=-1, keepdims=True)
    d = o - mu
    var = jnp.mean(d * d, axis=-1, keepdims=True)
    return g * _sigmoid(g) * (d * lax.rsqrt(var + LN_EPS))


def _ret_prompt_kernel(q_ref, k_ref, v_ref, g_ref, cos_ref, sin_ref, dm_ref, cr_ref, up_ref, cd_ref,
                       s0_ref, o_ref, s_ref):
    n_heads = s_ref.shape[1]
    scale = HEAD_DIM ** -0.5

    @pl.when(pl.program_id(1) == 0)
    def _():
        s_ref[...] = s0_ref[...]

    cos = cos_ref[...]
    sin = sin_ref[...]
    for hh in range(n_heads):
        sl = slice(hh * HEAD_DIM, (hh + 1) * HEAD_DIM)
        q = _rotate(q_ref[:, sl], cos, sin)
        k = _rotate(k_ref[:, sl], cos, sin) * scale
        qb = q.astype(BF16)
        kb = k.astype(BF16)
        vb = v_ref[:, sl].astype(BF16)
        s = s_ref[0, hh]
        sc = lax.dot_general(qb, kb, _NT, preferred_element_type=F32) * dm_ref[hh]
        o = jnp.dot(sc.astype(BF16), vb, preferred_element_type=F32)
        o = o + jnp.dot(qb, s.astype(BF16), preferred_element_type=F32) * cr_ref[hh]
        ku = (k * up_ref[hh]).astype(BF16)
        s_ref[0, hh] = s * cd_ref[hh] + lax.dot_general(ku, vb, _TN, preferred_element_type=F32)
        o_ref[:, sl] = _head_norm_gate(o, g_ref[:, sl]).astype(o_ref.dtype)


def _ret_prompt(h, cos, sin, tabs, s0, *, batch, seq, n_heads, col0):
    nc = seq // RET_CHUNK
    width = n_heads * HEAD_DIM
    hspec = lambda g: pl.BlockSpec((RET_CHUNK, width), lambda b, c: (b * nc + c, col0 + g))
    tspec = pl.BlockSpec((n_heads, RET_CHUNK, HEAD_DIM), lambda b, c: (0, 0, 0))
    sspec = pl.BlockSpec((1, n_heads, HEAD_DIM, HEAD_DIM), lambda b, c: (b, 0, 0, 0))
    return pl.pallas_call(
        _ret_prompt_kernel,
        grid=(batch, nc),
        in_specs=[hspec(0), hspec(1), hspec(2), hspec(3),
                  pl.BlockSpec((RET_CHUNK, HEAD_DIM), lambda b, c: (c, 0)),
                  pl.BlockSpec((RET_CHUNK, HEAD_DIM), lambda b, c: (c, 0)),
                  tspec, tspec, tspec, tspec, sspec],
        out_specs=[pl.BlockSpec((RET_CHUNK, width), lambda b, c: (b * nc + c, 0)), sspec],
        out_shape=[jax.ShapeDtypeStruct((batch * seq, width), BF16),
                   jax.ShapeDtypeStruct((batch, n_heads, HEAD_DIM, HEAD_DIM), F32)],
        compiler_params=_cparams("parallel", "arbitrary"),
        name="ret_prompt",
    )(h, h, h, h, cos, sin, *tabs, s0)


def _outproj_kernel(alpha, a_ref, r_ref, wa_ref, wr_ref, x_ref, g_ref, b_ref, of_ref, ob_ref):
    mix = jnp.dot(a_ref[...], wa_ref[0], preferred_element_type=F32)
    mix = mix + jnp.dot(r_ref[...], wr_ref[0], preferred_element_type=F32)
    y = _layer_norm_rows(alpha * x_ref[...] + mix, g_ref[0], b_ref[0])
    of_ref[...] = y
    ob_ref[...] = y.astype(BF16)


def _outproj_ln(attn, ret, w_out, layer, x, g, b, *, alpha, tm):
    m, ka = attn.shape
    d = x.shape[1]
    row = lambda width: pl.BlockSpec((tm, width), lambda i: (i, 0))
    vec = pl.BlockSpec((1, 1, d), lambda i: (layer, 0, 0))
    return pl.pallas_call(
        functools.partial(_outproj_kernel, alpha),
        grid=(m // tm,),
        in_specs=[row(ka), row(ka),
                  pl.BlockSpec((1, ka, d), lambda i: (layer, 0, 0)),
                  pl.BlockSpec((1, ka, d), lambda i: (layer, 1, 0)),
                  row(d), vec, vec],
        out_specs=[row(d), row(d)],
        out_shape=[jax.ShapeDtypeStruct((m, d), F32), jax.ShapeDtypeStruct((m, d), BF16)],
        compiler_params=_cparams("parallel"),
        name="outproj_ln",
    )(attn, ret, w_out, w_out, x, g, b)


def _ffn_up_prompt_kernel(tiles_per_seq, x_ref, wg_ref, wu_ref, cw_ref, cb_ref, buf_ref,
                          o_ref, cn_ref, carry_ref, wgb_ref, wub_ref):
    i = pl.program_id(1)
    tm = x_ref.shape[0]

    @pl.when(i == 0)
    def _():
        wgb_ref[...] = wg_ref[0].astype(BF16)
        wub_ref[...] = wu_ref[0].astype(BF16)

    x = x_ref[...]
    a = jnp.dot(x, wgb_ref[...], preferred_element_type=F32)
    u = jnp.dot(x, wub_ref[...], preferred_element_type=F32)
    first = (i % tiles_per_seq) == 0
    prev = jnp.where(first, buf_ref[0], carry_ref[6:8, :])
    row = lax.broadcasted_iota(jnp.int32, a.shape, 0)
    a1 = jnp.where(row == 0, prev[1:2, :], pltpu.roll(a, 1, 0))
    a2 = jnp.where(row == 0, prev[0:1, :], jnp.where(row == 1, prev[1:2, :], pltpu.roll(a, 2, 0)))
    cw = cw_ref[0]
    conv = cb_ref[0] + cw[0:1, :] * a2 + cw[1:2, :] * a1 + cw[2:3, :] * a
    o_ref[...] = (conv * _sigmoid(conv) * u).astype(o_ref.dtype)
    carry_ref[...] = a[tm - 8:tm, :]

    @pl.when((i % tiles_per_seq) == tiles_per_seq - 1)
    def _():
        cn_ref[0] = a[tm - (CONV_W - 1):tm, :]


def _ffn_up_prompt(x, w_gate, w_up, conv_w, conv_b, conv_buf, layer, *, batch, seq, tm, tn):
    m, d = x.shape
    nf = w_gate.shape[2]
    tps = seq // tm
    wspec = pl.BlockSpec((1, d, tn), lambda j, i: (layer, 0, j))
    return pl.pallas_call(
        functools.partial(_ffn_up_prompt_kernel, tps),
        grid=(nf // tn, m // tm),
        in_specs=[pl.BlockSpec((tm, d), lambda j, i: (i, 0)), wspec, wspec,
                  pl.BlockSpec((1, CONV_W, tn), lambda j, i: (layer, 0, j)),
                  pl.BlockSpec((1, 1, tn), lambda j, i: (layer, 0, j)),
                  pl.BlockSpec((1, CONV_W - 1, tn), lambda j, i: (i // tps, 0, j))],
        out_specs=[pl.BlockSpec((tm, tn), lambda j, i: (i, j)),
                   pl.BlockSpec((1, CONV_W - 1, tn), lambda j, i: (i // tps, 0, j))],
        out_shape=[jax.ShapeDtypeStruct((m, nf), BF16),
                   jax.ShapeDtypeStruct((batch, CONV_W - 1, nf), F32)],
        scratch_shapes=[pltpu.VMEM((8, tn), F32), pltpu.VMEM((d, tn), BF16), pltpu.VMEM((d, tn), BF16)],
        compiler_params=_cparams("parallel", "arbitrary"),
        name="ffn_up_prompt",
    )(x, w_gate, w_up, conv_w, conv_b, conv_buf)


def _ffn_up_sample_kernel(x_ref, wg_ref, wu_ref, cw_ref, cb_ref, b0_ref, b1_ref, o_ref, n0_ref, n1_ref):
    x = x_ref[...]
    a = jnp.dot(x, wg_ref[0].astype(BF16), preferred_element_type=F32)
    u = jnp.dot(x, wu_ref[0].astype(BF16), preferred_element_type=F32)
    cw = cw_ref[0]
    conv = cb_ref[0] + cw[0:1, :] * b0_ref[...] + cw[1:2, :] * b1_ref[...] + cw[2:3, :] * a
    o_ref[...] = (conv * _sigmoid(conv) * u).astype(o_ref.dtype)
    n0_ref[...] = b1_ref[...]
    n1_ref[...] = a


def _ffn_up_sample(x, w_gate, w_up, conv_w, conv_b, buf0, buf1, layer, *, tn):
    m, d = x.shape
    nf = w_gate.shape[2]
    wspec = pl.BlockSpec((1, d, tn), lambda j: (layer, 0, j))
    cspec = pl.BlockSpec((m, tn), lambda j: (0, j))
    return pl.pallas_call(
        _ffn_up_sample_kernel,
        grid=(nf // tn,),
        in_specs=[pl.BlockSpec((m, d), lambda j: (0, 0)), wspec, wspec,
                  pl.BlockSpec((1, CONV_W, tn), lambda j: (layer, 0, j)),
                  pl.BlockSpec((1, 1, tn), lambda j: (layer, 0, j)),
                  cspec, cspec],
        out_specs=[cspec, cspec, cspec],
        out_shape=[jax.ShapeDtypeStruct((m, nf), BF16),
                   jax.ShapeDtypeStruct((m, nf), F32),
                   jax.ShapeDtypeStruct((m, nf), F32)],
        compiler_params=_cparams("parallel"),
        name="ffn_up_sample",
    )(x, w_gate, w_up, conv_w, conv_b, buf0, buf1)


def _ffn_down_kernel(alpha, g_ref, w_ref, x_ref, lg_ref, lb_ref, o_ref, acc_ref):
    kk = pl.program_id(1)

    @pl.when(kk == 0)
    def _():
        acc_ref[...] = jnp.zeros_like(acc_ref)

    acc_ref[...] += jnp.dot(g_ref[...], w_ref[0], preferred_element_type=F32)

    @pl.when(kk == pl.num_programs(1) - 1)
    def _():
        o_ref[...] = _layer_norm_rows(alpha * x_ref[...] + acc_ref[...], lg_ref[0], lb_ref[0])


def _ffn_down_ln(g, w_down, layer, x, ln_g, ln_b, *, alpha, tm, tk):
    m, nf = g.shape
    d = x.shape[1]
    vec = pl.BlockSpec((1, 1, d), lambda i, kk: (layer, 0, 0))
    return pl.pallas_call(
        functools.partial(_ffn_down_kernel, alpha),
        grid=(m // tm, nf // tk),
        in_specs=[pl.BlockSpec((tm, tk), lambda i, kk: (i, kk)),
                  pl.BlockSpec((1, tk, d), lambda i, kk: (layer, kk, 0)),
                  pl.BlockSpec((tm, d), lambda i, kk: (i, 0)), vec, vec],
        out_specs=pl.BlockSpec((tm, d), lambda i, kk: (i, 0)),
        out_shape=jax.ShapeDtypeStruct((m, d), F32),
        scratch_shapes=[pltpu.VMEM((tm, d), F32)],
        compiler_params=_cparams("parallel", "arbitrary"),
        name="ffn_down_ln",
    )(g, w_down, x, ln_g, ln_b)


def _ple_kernel(x_ref, p_ref, wg_ref, wp_ref, of_ref, ob_ref):
    x = x_ref[...]
    gate = _sigmoid(jnp.dot(x.astype(BF16), wg_ref[0], preferred_element_type=F32))
    emb = jnp.dot(p_ref[0].astype(BF16), wp_ref[0], preferred_element_type=F32)
    y = x + gate * emb
    of_ref[...] = y
    ob_ref[...] = y.astype(BF16)


def _ple(x, p, w_ple_gate, w_ple, layer, *, tm):
    m, d = x.shape
    pd = p.shape[2]
    row = lambda width: pl.BlockSpec((tm, width), lambda i: (i, 0))
    return pl.pallas_call(
        _ple_kernel,
        grid=(m // tm,),
        in_specs=[row(d), pl.BlockSpec((1, tm, pd), lambda i: (layer, i, 0)),
                  pl.BlockSpec((1, d, d), lambda i: (layer, 0, 0)),
                  pl.BlockSpec((1, pd, d), lambda i: (layer, 0, 0))],
        out_specs=[row(d), row(d)],
        out_shape=[jax.ShapeDtypeStruct((m, d), F32), jax.ShapeDtypeStruct((m, d), BF16)],
        compiler_params=_cparams("parallel"),
        name="ple",
    )(x, p, w_ple_gate, w_ple)


KSUM_PAGES_PER_STEP = 8


def _block_ksum_kernel(ids_ref, *refs):
    del ids_ref
    page_refs, o_ref = refs[:-1], refs[-1]
    n_heads = o_ref.shape[2]
    ppb = len(page_refs) // o_ref.shape[1]
    for r, p_ref in enumerate(page_refs):
        x = p_ref[0, 0]
        part = jnp.sum(x.reshape(x.shape[0] // n_heads, n_heads, x.shape[1]), axis=0)
        if r % ppb == 0:
            o_ref[0, r // ppb] = part
        else:
            o_ref[0, r // ppb] += part


def _block_ksum(cache_k_rows, page_table_flat, pages_per_block, n_heads):
    depth, _, rows, hd = cache_k_rows.shape
    pps = KSUM_PAGES_PER_STEP
    n_pages = page_table_flat.shape[0]
    assert n_pages % pps == 0 and pps % pages_per_block == 0
    bps = pps // pages_per_block
    page_spec = lambda r: pl.BlockSpec((1, 1, rows, hd), lambda l, st, ids: (l, ids[pps * st + r], 0, 0))
    return pl.pallas_call(
        _block_ksum_kernel,
        grid_spec=pltpu.PrefetchScalarGridSpec(
            num_scalar_prefetch=1,
            grid=(depth, n_pages // pps),
            in_specs=[page_spec(r) for r in range(pps)],
            out_specs=pl.BlockSpec((1, bps, n_heads, hd), lambda l, st, ids: (l, st, 0, 0)),
        ),
        out_shape=jax.ShapeDtypeStruct((depth, n_pages // pages_per_block, n_heads, hd), F32),
        compiler_params=_cparams("parallel", "parallel"),
        name="block_ksum",
    )(page_table_flat, *([cache_k_rows] * pps))


def _sample_select_kernel(q_ref, ks_ref, o_ref):
    nblk, n_heads = ks_ref.shape[1], ks_ref.shape[2]
    prod = ks_ref[0] * q_ref[0][None] * (1.0 / MOBA_BLOCK)
    gate = jnp.sum(prod, axis=-1, keepdims=True)
    blk = lax.broadcasted_iota(jnp.int32, gate.shape, 0).astype(F32)
    for t in range(MOBA_TOPK):
        best = jnp.max(gate, axis=0, keepdims=True)
        idx = jnp.min(jnp.where(gate == best, blk, float(nblk)), axis=0, keepdims=True)
        o_ref[0, t] = jnp.broadcast_to(idx[0], (n_heads, 128)).astype(jnp.int32)
        gate = jnp.where(blk == idx, -jnp.inf, gate)


def _sample_select(q_heads, ksum_l):
    bs, nblk, n_heads, hd = ksum_l.shape
    return pl.pallas_call(
        _sample_select_kernel,
        grid=(bs,),
        in_specs=[pl.BlockSpec((1, n_heads, hd), lambda b: (b, 0, 0)),
                  pl.BlockSpec((1, nblk, n_heads, hd), lambda b: (b, 0, 0, 0))],
        out_specs=pl.BlockSpec((1, MOBA_TOPK, n_heads, 128), lambda b: (b, 0, 0, 0)),
        out_shape=jax.ShapeDtypeStruct((bs, MOBA_TOPK, n_heads, 128), jnp.int32),
        compiler_params=_cparams("parallel"),
        name="sample_select",
    )(q_heads, ksum_l)


def _sample_attn_kernel(layer, n_sel_pages, pid_ref, near_ref, q_ref, kn_ref, vn_ref, ck_hbm, cv_hbm,
                        bn_ref, bf_ref, o_ref, kbuf, vbuf, sem):
    b = pl.program_id(0)
    n_seq = pl.num_programs(0)
    n_heads = o_ref.shape[1]
    ppb, page = bn_ref.shape[1], bn_ref.shape[2]
    scale = HEAD_DIM ** -0.5

    def page_copies(seq, slot):
        copies = []
        for hh in range(n_heads):
            for j in range(n_sel_pages):
                pid = pid_ref[(seq * n_heads + hh) * n_sel_pages + j]
                copies.append(pltpu.make_async_copy(ck_hbm.at[layer, pid, :, hh, :], kbuf.at[slot, hh, j],
                                                    sem.at[slot]))
                copies.append(pltpu.make_async_copy(cv_hbm.at[layer, pid, :, hh, :], vbuf.at[slot, hh, j],
                                                    sem.at[slot]))
        return copies

    slot = lax.rem(b, 2)

    @pl.when(b == 0)
    def _():
        for cp in page_copies(0, 0):
            cp.start()

    @pl.when(b + 1 < n_seq)
    def _():
        for cp in page_copies(b + 1, 1 - slot):
            cp.start()

    for cp in page_copies(b, slot):
        cp.wait()

    for hh in range(n_heads):
        sl = slice(hh * HEAD_DIM, (hh + 1) * HEAD_DIM)
        q = q_ref[0, :, sl]
        qb = jnp.broadcast_to(q, (8, HEAD_DIM)).astype(BF16)
        ks = kbuf[slot, hh].reshape(n_sel_pages * page, HEAD_DIM).astype(BF16)
        vs = vbuf[slot, hh].reshape(n_sel_pages * page, HEAD_DIM).astype(BF16)
        s = lax.dot_general(qb, ks, _NT, preferred_element_type=F32)[0:1, :] * scale
        far_bias = bf_ref[hh, 0:1, 0:1]
        bias = []
        for j in range(n_sel_pages):
            near = near_ref[(b * n_heads + hh) * (n_sel_pages // ppb) + j // ppb]
            bias.append(jnp.where(near == 1, bn_ref[hh, j % ppb:j % ppb + 1, :], far_bias))
        s = s + jnp.concatenate(bias, axis=1)
        kn = kn_ref[0, :, sl]
        s_own = jnp.sum(q.astype(BF16).astype(F32) * kn.astype(BF16).astype(F32), axis=-1, keepdims=True)
        s_own = s_own * scale + bf_ref[hh, 1:2, 0:1]
        m = jnp.maximum(jnp.max(s, axis=-1, keepdims=True), s_own)
        p = jnp.exp(s - m)
        p_own = jnp.exp(s_own - m)
        l = jnp.sum(p, axis=-1, keepdims=True) + p_own
        pv = jnp.dot(jnp.broadcast_to(p, (8, p.shape[1])).astype(BF16), vs, preferred_element_type=F32)[0:1, :]
        acc = pv + p_own.astype(BF16).astype(F32) * vn_ref[0, :, sl].astype(BF16).astype(F32)
        o_ref[0, hh] = (acc / l).astype(o_ref.dtype)


def _sample_attn(hs3, cache_k, cache_v, layer, sel_pages, near, bias_near, bias_far, *, n_sel_pages):
    bs = hs3.shape[0]
    _, _, page, n_heads, hd = cache_k.shape
    width = n_heads * hd
    hspec = lambda g: pl.BlockSpec((1, 1, width), lambda b, pid, nr: (b, 0, g))
    whole = lambda a: pl.BlockSpec(a.shape, lambda b, pid, nr: (0,) * a.ndim)
    buf = pltpu.VMEM((2, n_heads, n_sel_pages, page, hd), cache_k.dtype)
    return pl.pallas_call(
        functools.partial(_sample_attn_kernel, layer, n_sel_pages),
        grid_spec=pltpu.PrefetchScalarGridSpec(
            num_scalar_prefetch=2,
            grid=(bs,),
            in_specs=[hspec(0), hspec(1), hspec(2),
                      pl.BlockSpec(memory_space=pl.ANY), pl.BlockSpec(memory_space=pl.ANY),
                      whole(bias_near), whole(bias_far)],
            out_specs=pl.BlockSpec((1, n_heads, 1, hd), lambda b, pid, nr: (b, 0, 0, 0)),
            scratch_shapes=[buf, buf, pltpu.SemaphoreType.DMA((2,))],
        ),
        out_shape=jax.ShapeDtypeStruct((bs, n_heads, 1, hd), BF16),
        compiler_params=_cparams("arbitrary"),
        name="sample_attn",
    )(sel_pages, near, hs3, hs3, hs3, cache_k, cache_v, bias_near, bias_far)


def _ret_sample_kernel(q_ref, k_ref, v_ref, g_ref, cos_ref, sin_ref, dec_ref, s0_ref, o_ref, s_ref):
    n_heads = s_ref.shape[1]
    scale = HEAD_DIM ** -0.5
    cos = cos_ref[...]
    sin = sin_ref[...]
    rows = lax.broadcasted_iota(jnp.int32, (HEAD_DIM, HEAD_DIM), 0)
    cols = lax.broadcasted_iota(jnp.int32, (HEAD_DIM, HEAD_DIM), 1)
    eye = jnp.where(rows == cols, 1.0, 0.0)
    for hh in range(n_heads):
        sl = slice(hh * HEAD_DIM, (hh + 1) * HEAD_DIM)
        q = _rotate(q_ref[0, :, sl], cos, sin)
        k = _rotate(k_ref[0, :, sl], cos, sin) * scale
        v = v_ref[0, :, sl]
        g = g_ref[0, :, sl]
        dec = dec_ref[hh:hh + 1, :]
        q_col = jnp.sum(eye * q, axis=-1, keepdims=True)
        k_col = jnp.sum(eye * k, axis=-1, keepdims=True)
        s = s0_ref[0, hh]
        qk = jnp.sum(q * k, axis=-1, keepdims=True)
        o = qk * v + jnp.sum(q_col * s, axis=0, keepdims=True) * dec
        s_ref[0, hh] = s * dec + k_col * v
        o_ref[0, :, sl] = _head_norm_gate(o, g).astype(o_ref.dtype)


def _ret_sample(hs, cos, sin, decay, s0, *, n_heads, col0):
    bs = hs.shape[0]
    width = n_heads * HEAD_DIM
    hspec = lambda g: pl.BlockSpec((1, 1, width), lambda b: (b, 0, col0 + g))
    vspec = pl.BlockSpec((1, HEAD_DIM), lambda b: (0, 0))
    sspec = pl.BlockSpec((1, n_heads, HEAD_DIM, HEAD_DIM), lambda b: (b, 0, 0, 0))
    return pl.pallas_call(
        _ret_sample_kernel,
        grid=(bs,),
        in_specs=[hspec(0), hspec(1), hspec(2), hspec(3), vspec, vspec,
                  pl.BlockSpec((n_heads, HEAD_DIM), lambda b: (0, 0)), sspec],
        out_specs=[pl.BlockSpec((1, 1, width), lambda b: (b, 0, 0)), sspec],
        out_shape=[jax.ShapeDtypeStruct((bs, 1, width), BF16),
                   jax.ShapeDtypeStruct((bs, n_heads, HEAD_DIM, HEAD_DIM), F32)],
        compiler_params=_cparams("parallel"),
        name="ret_sample",
    )(hs, hs, hs, hs, cos, sin, decay, s0)


def _t5_bucket(rel):
    n = jnp.maximum(rel, 0)
    max_exact = NUM_BUCKETS // 2
    nf = jnp.maximum(n, max_exact).astype(F32)
    large = max_exact + (jnp.log(nf / max_exact) / math.log(MAX_DISTANCE / max_exact)
                         * (NUM_BUCKETS - max_exact)).astype(jnp.int32)
    return jnp.where(n < max_exact, n, jnp.minimum(large, NUM_BUCKETS - 1))


def _rope_tables(pos):
    half = HEAD_DIM // 2
    inv = ROPE_BASE ** (-jnp.arange(half, dtype=F32) / half)
    ang = pos[:, None] * inv[None, :]
    cos = jnp.cos(ang)
    sin = jnp.sin(ang)
    return jnp.concatenate([cos, cos], -1), jnp.concatenate([-sin, sin], -1)


def _decay_tables(n_heads, chunk):
    lg = jnp.log(1.0 - 2.0 ** (-5.0 - jnp.arange(n_heads, dtype=F32)))
    i = jnp.arange(chunk, dtype=F32)
    diff = i[:, None] - i[None, :]
    dmask = jnp.where(diff >= 0, jnp.exp(lg[:, None, None] * jnp.maximum(diff, 0.0)), 0.0)
    full = lambda col: jnp.broadcast_to(col[..., None], (n_heads, chunk, HEAD_DIM))
    cross = full(jnp.exp(lg[:, None] * (i + 1.0)))
    upd = full(jnp.exp(lg[:, None] * (chunk - 1.0 - i)))
    cdec = jnp.broadcast_to(jnp.exp(lg * chunk)[:, None, None], (n_heads, chunk, HEAD_DIM))
    return dmask, cross, upd, cdec


def kernel(x_prompt, x_sample, cache_k, cache_v, state_ret, state_conv, page_table, p_prompt, p_sample,
           rel_bias, w_in, w_out, ln1_g, ln1_b, w_gate, w_up, conv_w, conv_b, w_down, ln2_g, ln2_b,
           w_ple, w_ple_gate):
    bp, seq, d = x_prompt.shape
    bs, dec_seq, _ = x_sample.shape
    depth, n_pool, page, n_heads, hd = cache_k.shape
    n_pages = page_table.shape[1]
    past_len = n_pages * page
    nf = w_gate.shape[2]
    attn_w = n_heads * hd
    assert hd == HEAD_DIM and dec_seq == 1 and MOBA_BLOCK % page == 0
    assert seq % MOBA_BLOCK == 0 and past_len % MOBA_BLOCK == 0 and past_len // MOBA_BLOCK >= MOBA_TOPK
    assert w_in.shape[2] == 7 * attn_w and d == 2 * attn_w and RET_CHUNK == HEAD_DIM
    assert int(np.floor(np.log(np.float32(MOBA_BLOCK + 1) / 16) / math.log(MAX_DISTANCE / 16) * 16)) >= 15
    alpha = (2 * depth) ** 0.25
    ppb = MOBA_BLOCK // page
    n_past_blocks = past_len // MOBA_BLOCK
    mp = bp * seq

    w_out_b, w_down_b, w_ple_b, w_pg_b = (w.astype(BF16) for w in (w_out, w_down, w_ple, w_ple_gate))
    vec3 = lambda a: a.reshape(depth, 1, a.shape[-1])
    ln1_g3, ln1_b3, ln2_g3, ln2_b3, conv_b3 = map(vec3, (ln1_g, ln1_b, ln2_g, ln2_b, conv_b))

    def bias_of(rel):
        onehot = (_t5_bucket(rel)[..., None] == jnp.arange(NUM_BUCKETS)).astype(F32)
        return jnp.einsum("...b,bh->h...", onehot, rel_bias.astype(F32), precision=lax.Precision.HIGHEST)

    ii = jnp.arange(MOBA_BLOCK, dtype=jnp.int32)
    far = rel_bias[NUM_BUCKETS - 1].astype(F32)
    bias_diag = bias_of(ii[:, None] - ii[None, :]) * LOG2E
    bias_adj = bias_of(MOBA_BLOCK + ii[:, None] - ii[None, :]) * LOG2E
    bias_far = jnp.broadcast_to((far * LOG2E)[:, None, None], (n_heads, 1, 128))
    bias_near_s = bias_of(MOBA_BLOCK - ii).reshape(n_heads, ppb, page)
    bias_far_s = jnp.stack([jnp.broadcast_to(far[:, None], (n_heads, 128)),
                            jnp.broadcast_to(rel_bias[0].astype(F32)[:, None], (n_heads, 128))], axis=1)

    cos_p, sin_p = _rope_tables(jnp.arange(seq, dtype=F32))
    cos_s, sin_s = _rope_tables(jnp.full((1,), float(past_len), F32))
    tabs = _decay_tables(n_heads, RET_CHUNK)
    decay_s = jnp.broadcast_to(
        (1.0 - 2.0 ** (-5.0 - jnp.arange(n_heads, dtype=F32)))[:, None], (n_heads, HEAD_DIM))
    decay_s = jnp.exp(jnp.log(decay_s))

    ck = cache_k.reshape(depth, n_pool, page * n_heads, hd)
    ksum = _block_ksum(ck, page_table.reshape(-1), ppb, n_heads)
    ksum = ksum.reshape(depth, bs, n_past_blocks, n_heads, hd)

    xp_f = x_prompt.reshape(mp, d)
    xp_b = xp_f.astype(BF16)
    xs_f = x_sample.reshape(bs, d)
    xs_b = xs_f.astype(BF16)
    s0_prompt = jnp.zeros((bp, n_heads, hd, hd), F32)
    conv0_prompt = jnp.zeros((bp, CONV_W - 1, nf), F32)

    pp3 = p_prompt.reshape(depth, mp, -1)
    ps3 = p_sample.reshape(depth, bs, -1)
    outs = {k: [] for k in ("ks", "vs", "rp", "rs", "cp", "cs")}
    kp_all = vp_all = None
    for l in range(depth):
        qp = _matmul(xp_b, w_in, l, col0=0, n=attn_w, tm=1024, tn=1024)
        tm_kv = 1024 if l else 512
        kp_all = _matmul_stacked(xp_b, w_in, l, kp_all, col0=attn_w, n=attn_w, tm=tm_kv, tn=1024)
        vp_all = _matmul_stacked(xp_b, w_in, l, vp_all, col0=2 * attn_w, n=attn_w, tm=tm_kv, tn=1024)
        hr = _matmul(xp_b, w_in, l, col0=3 * attn_w, n=4 * attn_w, tm=1024, tn=1024)
        attn = _moba_prompt(qp, kp_all, vp_all, l, bias_diag, bias_adj, bias_far,
                            batch=bp, seq=seq, n_heads=n_heads)
        ret, s_new = _ret_prompt(hr, cos_p, sin_p, tabs, s0_prompt, batch=bp, seq=seq, n_heads=n_heads, col0=0)
        outs["rp"].append(s_new)
        x1_f, x1_b = _outproj_ln(attn, ret, w_out_b, l, xp_f, ln1_g3, ln1_b3, alpha=alpha, tm=512)
        gact, conv_new = _ffn_up_prompt(x1_b, w_gate, w_up, conv_w, conv_b3, conv0_prompt, l,
                                        batch=bp, seq=seq, tm=1024, tn=512)
        outs["cp"].append(conv_new)
        x2 = _ffn_down_ln(gact, w_down_b, l, x1_f, ln2_g3, ln2_b3, alpha=alpha, tm=512, tk=nf // 2)
        xp_f, xp_b = _ple(x2, pp3, w_pg_b, w_ple_b, l, tm=512)

        hs = _matmul(xs_b, w_in, l, col0=0, n=w_in.shape[2], tm=bs, tn=1024)
        outs["ks"].append(hs[:, attn_w:2 * attn_w].reshape(bs, 1, n_heads, hd))
        outs["vs"].append(hs[:, 2 * attn_w:3 * attn_w].reshape(bs, 1, n_heads, hd))
        hs3 = hs.reshape(bs, 1, -1)
        picks = _sample_select(hs[:, :attn_w].reshape(bs, n_heads, hd), ksum[l])
        blocks = jnp.transpose(picks[..., 0], (0, 2, 1))
        sel_pos = blocks[..., None] * ppb + jnp.arange(ppb, dtype=jnp.int32)
        sel_pages = page_table[jnp.arange(bs)[:, None, None, None], sel_pos]
        near = (blocks == n_past_blocks - 1).astype(jnp.int32)
        attn_s = _sample_attn(hs3, cache_k, cache_v, l, sel_pages.reshape(-1), near.reshape(-1),
                              bias_near_s, bias_far_s, n_sel_pages=MOBA_TOPK * ppb)
        ret_s, s_new_s = _ret_sample(hs3, cos_s, sin_s, decay_s, state_ret[l], n_heads=n_heads, col0=3)
        outs["rs"].append(s_new_s)
        x1s_f, x1s_b = _outproj_ln(attn_s.reshape(bs, attn_w), ret_s.reshape(bs, attn_w), w_out_b, l, xs_f,
                                   ln1_g3, ln1_b3, alpha=alpha, tm=bs)
        gact_s, c0, c1 = _ffn_up_sample(x1s_b, w_gate, w_up, conv_w, conv_b3,
                                        state_conv[l, :, 0], state_conv[l, :, 1], l, tn=512)
        outs["cs"].append(jnp.stack([c0, c1], axis=1))
        x2s = _ffn_down_ln(gact_s, w_down_b, l, x1s_f, ln2_g3, ln2_b3, alpha=alpha, tm=bs, tk=nf // 4)
        xs_f, xs_b = _ple(x2s, ps3, w_pg_b, w_ple_b, l, tm=bs)

    stack = lambda key: jnp.stack(outs[key])
    return (xp_f.reshape(bp, seq, d), xs_f.reshape(bs, 1, d),
            kp_all.reshape(depth, bp, seq, n_heads, hd), vp_all.reshape(depth, bp, seq, n_heads, hd),
            stack("ks"), stack("vs"), stack("rp"), stack("rs"), stack("cp"), stack("cs"))
```

```python
import functools
import math

import jax
import jax.numpy as jnp
import numpy as np
from jax import lax
from jax.experimental import pallas as pl
from jax.experimental.pallas import tpu as pltpu

HEAD_DIM = 128
MOBA_BLOCK = 256
MOBA_TOPK = 3
RET_CHUNK = 128
ROPE_BASE = 10000.0
NUM_BUCKETS = 32
MAX_DISTANCE = 128
CONV_W = 3
LN_EPS = 1e-5

VMEM_LIMIT_BYTES = 56 * 1024 * 1024
NEG_BIG = -1e30
LOG2E = math.log2(math.e)

BF16 = jnp.bfloat16
F32 = jnp.float32

_NT = (((1,), (1,)), ((), ()))
_TN = (((0,), (0,)), ((), ()))


def _cparams(*sem):
    return pltpu.CompilerParams(dimension_semantics=sem, vmem_limit_bytes=VMEM_LIMIT_BYTES)


def _sigmoid(x):
    return 1.0 / (1.0 + jnp.exp(-x))


def _layer_norm_rows(y, g, b):
    mu = jnp.mean(y, axis=-1, keepdims=True)
    d = y - mu
    var = jnp.mean(d * d, axis=-1, keepdims=True)
    return d * lax.rsqrt(var + LN_EPS) * g + b


def _split_bf16(a):
    hi = a.astype(BF16)
    lo = (a - hi.astype(F32)).astype(BF16)
    return hi, lo


def _dot_nt_precise(a, b):
    ah, al = _split_bf16(a)
    bh, bl = _split_bf16(b)
    dot = functools.partial(lax.dot_general, dimension_numbers=_NT, preferred_element_type=F32)
    return dot(ah, bh) + (dot(ah, bl) + dot(al, bh))


def _mm_kernel(x_ref, w_ref, o_ref, wb_ref):
    @pl.when(pl.program_id(1) == 0)
    def _():
        wb_ref[...] = w_ref[0].astype(BF16)

    o_ref[...] = jnp.dot(x_ref[...], wb_ref[...], preferred_element_type=F32).astype(o_ref.dtype)


def _matmul(x, w, layer, *, col0, n, tm, tn, out_dtype=F32):
    m, k = x.shape
    assert col0 % tn == 0 and n % tn == 0
    return pl.pallas_call(
        _mm_kernel,
        grid=(n // tn, m // tm),
        in_specs=[
            pl.BlockSpec((tm, k), lambda j, i: (i, 0)),
            pl.BlockSpec((1, k, tn), lambda j, i: (layer, 0, col0 // tn + j)),
        ],
        out_specs=pl.BlockSpec((tm, tn), lambda j, i: (i, j)),
        out_shape=jax.ShapeDtypeStruct((m, n), out_dtype),
        scratch_shapes=[pltpu.VMEM((k, tn), BF16)],
        compiler_params=_cparams("parallel", "arbitrary"),
        name="proj_in",
    )(x, w)


def _mm_stacked_kernel(slab, x_ref, w_ref, *rest):
    o_ref, wb_ref = rest[-2:]

    @pl.when(pl.program_id(1) == 0)
    def _():
        wb_ref[...] = w_ref[0].astype(BF16)

    for other in range(o_ref.shape[0]):
        if other != slab:
            o_ref[other] = jnp.zeros(o_ref.shape[1:], o_ref.dtype)
    o_ref[slab] = jnp.dot(x_ref[...], wb_ref[...], preferred_element_type=F32)


def _matmul_stacked(x, w, layer, stack, *, col0, n, tm, tn):
    m, k = x.shape
    depth = w.shape[0]
    assert col0 % tn == 0 and n % tn == 0
    in_specs = [pl.BlockSpec((tm, k), lambda j, i: (i, 0)),
                pl.BlockSpec((1, k, tn), lambda j, i: (layer, 0, col0 // tn + j))]
    if stack is None:
        operands, aliases, slab = [x, w], {}, layer
        out_spec = pl.BlockSpec((depth, tm, tn), lambda j, i: (0, i, j))
    else:
        in_specs.append(pl.BlockSpec(memory_space=pl.ANY))
        operands, aliases, slab = [x, w, stack], {2: 0}, 0
        out_spec = pl.BlockSpec((1, tm, tn), lambda j, i: (layer, i, j))
    return pl.pallas_call(
        functools.partial(_mm_stacked_kernel, slab),
        grid=(n // tn, m // tm),
        in_specs=in_specs,
        out_specs=out_spec,
        out_shape=jax.ShapeDtypeStruct((depth, m, n), F32),
        scratch_shapes=[pltpu.VMEM((k, tn), BF16)],
        input_output_aliases=aliases,
        compiler_params=_cparams("parallel", "arbitrary"),
        name="proj_kv",
    )(*operands)


def _moba_select(cc, qf, km_ref):
    tq = qf.shape[0]
    nbp = km_ref.shape[0]
    gate = _dot_nt_precise(km_ref[...], qf)
    blk = lax.broadcasted_iota(jnp.int32, (nbp, tq), 0)
    eligible = blk < cc
    gate = jnp.where(eligible, gate, -jnp.inf)
    rank = jnp.zeros((nbp, tq), F32)
    for mth in range(cc):
        gm = gate[mth:mth + 1, :]
        beats = jnp.logical_or(gm > gate, jnp.logical_and(gm == gate, mth < blk))
        rank = rank + jnp.where(beats, 1.0, 0.0)
    sel_t = jnp.where(jnp.logical_and(eligible, rank < MOBA_TOPK), 1.0, 0.0)
    sel_pad = jnp.concatenate([sel_t, jnp.zeros((128 - nbp, tq), F32)], axis=0).astype(BF16)
    rows = lax.broadcasted_iota(jnp.int32, (tq, tq), 0)
    cols = lax.broadcasted_iota(jnp.int32, (tq, tq), 1)
    eye = jnp.where(rows == cols, 1.0, 0.0).astype(BF16)
    return lax.dot_general(eye, sel_pad, _NT, preferred_element_type=F32)


def _moba_tile(cc, q_ref, bd_ref, ba_ref, bf_ref, o_ref, kb_ref, vb_ref, km_ref):
    tq = q_ref.shape[0]
    scale2 = HEAD_DIM ** -0.5 * LOG2E
    nk = (cc + 1) * MOBA_BLOCK
    rows = lax.broadcasted_iota(jnp.int32, (tq, MOBA_BLOCK), 0)
    cols = lax.broadcasted_iota(jnp.int32, (tq, MOBA_BLOCK), 1)
    for hp in range(q_ref.shape[1] // HEAD_DIM):
        sl = slice(hp * HEAD_DIM, (hp + 1) * HEAD_DIM)
        qf = q_ref[:, sl]
        s = lax.dot_general(qf.astype(BF16), kb_ref[0:nk, sl], _NT, preferred_element_type=F32) * scale2
        sel = _moba_select(cc, qf, km_ref.at[hp]) if cc > MOBA_TOPK else None
        far_bias = bf_ref[hp, 0:1, 0:1]
        pieces = []
        for n in range(cc + 1):
            sn = s[:, n * MOBA_BLOCK:(n + 1) * MOBA_BLOCK]
            if n == cc:
                sn = jnp.where(cols <= rows, sn + bd_ref[hp], NEG_BIG)
            elif n == cc - 1:
                sn = sn + ba_ref[hp]
            else:
                sn = sn + far_bias
            if n < cc and sel is not None:
                sn = jnp.where(sel[:, n:n + 1] > 0.5, sn, NEG_BIG)
            pieces.append(sn)
        s = jnp.concatenate(pieces, axis=1) if cc else pieces[0]
        m = jnp.max(s, axis=-1, keepdims=True)
        p = jnp.exp2(s - m)
        l = jnp.sum(p, axis=-1, keepdims=True)
        acc = jnp.dot(p.astype(BF16), vb_ref[0:nk, sl], preferred_element_type=F32)
        o_ref[:, sl] = (acc / l).astype(o_ref.dtype)


def _moba_prompt_kernel(q_ref, k_ref, v_ref, bd_ref, ba_ref, bf_ref, o_ref, kb_ref, vb_ref, km_ref):
    c = pl.program_id(2)
    nb = k_ref.shape[1] // MOBA_BLOCK

    @pl.when(c == 0)
    def _():
        kb_ref[...] = k_ref[0].astype(BF16)
        vb_ref[...] = v_ref[0].astype(BF16)
        km_ref[...] = jnp.zeros_like(km_ref)
        for hp in range(km_ref.shape[0]):
            sl = slice(hp * HEAD_DIM, (hp + 1) * HEAD_DIM)
            for n in range(nb):
                km_ref[hp, n:n + 1, :] = jnp.mean(k_ref[0, n * MOBA_BLOCK:(n + 1) * MOBA_BLOCK, sl],
                                                  axis=0, keepdims=True)

    for cc in range(nb):
        pl.when(c == cc)(functools.partial(_moba_tile, cc, q_ref, bd_ref, ba_ref, bf_ref, o_ref,
                                           kb_ref, vb_ref, km_ref))


MOBA_HEADS_PER_STEP = 2


def _moba_prompt(q, k, v, layer, bias_diag, bias_adj, bias_far, *, batch, seq, n_heads):
    nq = seq // MOBA_BLOCK
    nbias = bias_far.shape[1]
    hps = MOBA_HEADS_PER_STEP
    width = hps * HEAD_DIM
    kvspec = pl.BlockSpec((1, seq, width), lambda b, hh, qi: (layer, b, hh))
    bspec = pl.BlockSpec((hps, MOBA_BLOCK, MOBA_BLOCK), lambda b, hh, qi: (hh, 0, 0))
    return pl.pallas_call(
        _moba_prompt_kernel,
        grid=(batch, n_heads // hps, nq),
        in_specs=[
            pl.BlockSpec((MOBA_BLOCK, width), lambda b, hh, qi: (b * nq + qi, hh)),
            kvspec, kvspec, bspec, bspec,
            pl.BlockSpec((hps, nbias, 128), lambda b, hh, qi: (hh, 0, 0)),
        ],
        out_specs=pl.BlockSpec((MOBA_BLOCK, width), lambda b, hh, qi: (b * nq + qi, hh)),
        out_shape=jax.ShapeDtypeStruct((batch * seq, n_heads * HEAD_DIM), BF16),
        scratch_shapes=[
            pltpu.VMEM((seq, width), BF16),
            pltpu.VMEM((seq, width), BF16),
            pltpu.VMEM((hps, max(16, seq // MOBA_BLOCK), HEAD_DIM), F32),
        ],
        compiler_params=_cparams("parallel", "parallel", "arbitrary"),
        name="moba_prompt",
    )(q, k, v, bias_diag, bias_adj, bias_far)


def _rotate(x, cos, sin_signed):
    return x * cos + pltpu.roll(x, HEAD_DIM // 2, 1) * sin_signed


def _head_norm_gate(o, g):
    mu = jnp.mean(o, axis=-1, keepdims=True)
    d = o - mu
    var = jnp.mean(d * d, axis=-1, keepdims=True)
    return g * _sigmoid(g) * (d * lax.rsqrt(var + LN_EPS))


def _ret_prompt_kernel(q_ref, k_ref, v_ref, g_ref, cos_ref, sin_ref, dm_ref, cr_ref, up_ref, cd_ref,
                       s0_ref, o_ref, s_ref):
    n_heads = s_ref.shape[1]
    scale = HEAD_DIM ** -0.5

    @pl.when(pl.program_id(1) == 0)
    def _():
        s_ref[...] = s0_ref[...]

    cos = cos_ref[...]
    sin = sin_ref[...]
    for hh in range(n_heads):
        sl = slice(hh * HEAD_DIM, (hh + 1) * HEAD_DIM)
        q = _rotate(q_ref[:, sl], cos, sin)
        k = _rotate(k_ref[:, sl], cos, sin) * scale
        qb = q.astype(BF16)
        kb = k.astype(BF16)
        vb = v_ref[:, sl].astype(BF16)
        s = s_ref[0, hh]
        sc = lax.dot_general(qb, kb, _NT, preferred_element_type=F32) * dm_ref[hh]
        o = jnp.dot(sc.astype(BF16), vb, preferred_element_type=F32)
        o = o + jnp.dot(qb, s.astype(BF16), preferred_element_type=F32) * cr_ref[hh]
        ku = (k * up_ref[hh]).astype(BF16)
        s_ref[0, hh] = s * cd_ref[hh] + lax.dot_general(ku, vb, _TN, preferred_element_type=F32)
        o_ref[:, sl] = _head_norm_gate(o, g_ref[:, sl]).astype(o_ref.dtype)


def _ret_prompt(h, cos, sin, tabs, s0, *, batch, seq, n_heads, col0):
    nc = seq // RET_CHUNK
    width = n_heads * HEAD_DIM
    hspec = lambda g: pl.BlockSpec((RET_CHUNK, width), lambda b, c: (b * nc + c, col0 + g))
    tspec = pl.BlockSpec((n_heads, RET_CHUNK, HEAD_DIM), lambda b, c: (0, 0, 0))
    sspec = pl.BlockSpec((1, n_heads, HEAD_DIM, HEAD_DIM), lambda b, c: (b, 0, 0, 0))
    return pl.pallas_call(
        _ret_prompt_kernel,
        grid=(batch, nc),
        in_specs=[hspec(0), hspec(1), hspec(2), hspec(3),
                  pl.BlockSpec((RET_CHUNK, HEAD_DIM), lambda b, c: (c, 0)),
                  pl.BlockSpec((RET_CHUNK, HEAD_DIM), lambda b, c: (c, 0)),
                  tspec, tspec, tspec, tspec, sspec],
        out_specs=[pl.BlockSpec((RET_CHUNK, width), lambda b, c: (b * nc + c, 0)), sspec],
        out_shape=[jax.ShapeDtypeStruct((batch * seq, width), BF16),
                   jax.ShapeDtypeStruct((batch, n_heads, HEAD_DIM, HEAD_DIM), F32)],
        compiler_params=_cparams("parallel", "arbitrary"),
        name="ret_prompt",
    )(h, h, h, h, cos, sin, *tabs, s0)


def _outproj_kernel(alpha, a_ref, r_ref, wa_ref, wr_ref, x_ref, g_ref, b_ref, of_ref, ob_ref, wb_ref):
    @pl.when(pl.program_id(0) == 0)
    def _():
        wb_ref[0] = wa_ref[0].astype(BF16)
        wb_ref[1] = wr_ref[0].astype(BF16)

    mix = jnp.dot(a_ref[...], wb_ref[0], preferred_element_type=F32)
    mix = mix + jnp.dot(r_ref[...], wb_ref[1], preferred_element_type=F32)
    y = _layer_norm_rows(alpha * x_ref[...] + mix, g_ref[0], b_ref[0])
    of_ref[...] = y
    ob_ref[...] = y.astype(BF16)


def _outproj_ln(attn, ret, w_out, layer, x, g, b, *, alpha, tm):
    m, ka = attn.shape
    d = x.shape[1]
    row = lambda width: pl.BlockSpec((tm, width), lambda i: (i, 0))
    vec = pl.BlockSpec((1, 1, d), lambda i: (layer, 0, 0))
    return pl.pallas_call(
        functools.partial(_outproj_kernel, alpha),
        grid=(m // tm,),
        in_specs=[row(ka), row(ka),
                  pl.BlockSpec((1, ka, d), lambda i: (layer, 0, 0), pipeline_mode=pl.Buffered(1)),
                  pl.BlockSpec((1, ka, d), lambda i: (layer, 1, 0), pipeline_mode=pl.Buffered(1)),
                  row(d), vec, vec],
        out_specs=[row(d), row(d)],
        out_shape=[jax.ShapeDtypeStruct((m, d), F32), jax.ShapeDtypeStruct((m, d), BF16)],
        scratch_shapes=[pltpu.VMEM((2, ka, d), BF16)],
        compiler_params=_cparams("arbitrary"),
        name="outproj_ln",
    )(attn, ret, w_out, w_out, x, g, b)


KSUM_PAGES_PER_STEP = 16


def _ffn_up_prompt_kernel(tiles_per_seq, n_pages, ids_ref, x_ref, wg_ref, wu_ref, cw_ref, cb_ref, buf_ref,
                          *rest):
    del ids_ref
    page_refs = rest[:n_pages]
    o_ref, cn_ref, ks_ref, carry_ref, wgb_ref, wub_ref = rest[n_pages:]
    i = pl.program_id(1)
    tm = x_ref.shape[0]

    @pl.when(i == 0)
    def _():
        wgb_ref[...] = wg_ref[0].astype(BF16)
        wub_ref[...] = wu_ref[0].astype(BF16)

    x = x_ref[...]
    a = jnp.dot(x, wgb_ref[...], preferred_element_type=F32)
    u = jnp.dot(x, wub_ref[...], preferred_element_type=F32)
    first = (i % tiles_per_seq) == 0
    prev = jnp.where(first, buf_ref[0], carry_ref[6:8, :])
    row = lax.broadcasted_iota(jnp.int32, a.shape, 0)
    a1 = jnp.where(row == 0, prev[1:2, :], pltpu.roll(a, 1, 0))
    a2 = jnp.where(row == 0, prev[0:1, :], jnp.where(row == 1, prev[1:2, :], pltpu.roll(a, 2, 0)))
    cw = cw_ref[0]
    conv = cb_ref[0] + cw[0:1, :] * a2 + cw[1:2, :] * a1 + cw[2:3, :] * a
    o_ref[...] = (conv * _sigmoid(conv) * u).astype(o_ref.dtype)
    carry_ref[...] = a[tm - 8:tm, :]

    @pl.when((i % tiles_per_seq) == tiles_per_seq - 1)
    def _():
        cn_ref[0] = a[tm - (CONV_W - 1):tm, :]

    n_heads = ks_ref.shape[1]
    ppb = n_pages // ks_ref.shape[0]
    for r, p_ref in enumerate(page_refs):
        pg = p_ref[0, 0]
        part = jnp.sum(pg.reshape(pg.shape[0] // n_heads, n_heads, pg.shape[1]), axis=0)
        if r % ppb == 0:
            ks_ref[r // ppb] = part
        else:
            ks_ref[r // ppb] += part


def _ffn_up_prompt(x, w_gate, w_up, conv_w, conv_b, conv_buf, layer, page_table_flat, cache_k_rows, *,
                   batch, seq, tm, tn, pages_per_block, n_heads):
    m, d = x.shape
    nf = w_gate.shape[2]
    tps = seq // tm
    n_i = m // tm
    _, _, rows, hd = cache_k_rows.shape
    pps = KSUM_PAGES_PER_STEP
    n_pages = page_table_flat.shape[0]
    n_groups = n_pages // pps
    bps = pps // pages_per_block
    assert n_pages % pps == 0 and pps % pages_per_block == 0 and n_groups <= (nf // tn) * n_i
    group = lambda j, i: jnp.minimum(j * n_i + i, n_groups - 1)
    page_spec = lambda r: pl.BlockSpec(
        (1, 1, rows, hd), lambda j, i, ids: (layer, ids[pps * group(j, i) + r], 0, 0))
    wspec = pl.BlockSpec((1, d, tn), lambda j, i, ids: (layer, 0, j))
    return pl.pallas_call(
        functools.partial(_ffn_up_prompt_kernel, tps, pps),
        grid_spec=pltpu.PrefetchScalarGridSpec(
            num_scalar_prefetch=1,
            grid=(nf // tn, n_i),
            in_specs=[pl.BlockSpec((tm, d), lambda j, i, ids: (i, 0)), wspec, wspec,
                      pl.BlockSpec((1, CONV_W, tn), lambda j, i, ids: (layer, 0, j)),
                      pl.BlockSpec((1, 1, tn), lambda j, i, ids: (layer, 0, j)),
                      pl.BlockSpec((1, CONV_W - 1, tn), lambda j, i, ids: (i // tps, 0, j))]
            + [page_spec(r) for r in range(pps)],
            out_specs=[pl.BlockSpec((tm, tn), lambda j, i, ids: (i, j)),
                       pl.BlockSpec((1, CONV_W - 1, tn), lambda j, i, ids: (i // tps, 0, j)),
                       pl.BlockSpec((bps, n_heads, hd), lambda j, i, ids: (group(j, i), 0, 0))],
            scratch_shapes=[pltpu.VMEM((8, tn), F32), pltpu.VMEM((d, tn), BF16), pltpu.VMEM((d, tn), BF16)],
        ),
        out_shape=[jax.ShapeDtypeStruct((m, nf), BF16),
                   jax.ShapeDtypeStruct((batch, CONV_W - 1, nf), F32),
                   jax.ShapeDtypeStruct((n_pages // pages_per_block, n_heads, hd), F32)],
        compiler_params=_cparams("arbitrary", "arbitrary"),
        name="ffn_up_prompt",
    )(page_table_flat, x, w_gate, w_up, conv_w, conv_b, conv_buf, *([cache_k_rows] * pps))


def _ffn_up_sample_kernel(x_ref, wg_ref, wu_ref, cw_ref, cb_ref, b0_ref, b1_ref, o_ref, n0_ref, n1_ref):
    x = x_ref[...]
    a = jnp.dot(x, wg_ref[0].astype(BF16), preferred_element_type=F32)
    u = jnp.dot(x, wu_ref[0].astype(BF16), preferred_element_type=F32)
    cw = cw_ref[0]
    conv = cb_ref[0] + cw[0:1, :] * b0_ref[...] + cw[1:2, :] * b1_ref[...] + cw[2:3, :] * a
    o_ref[...] = (conv * _sigmoid(conv) * u).astype(o_ref.dtype)
    n0_ref[...] = b1_ref[...]
    n1_ref[...] = a


def _ffn_up_sample(x, w_gate, w_up, conv_w, conv_b, buf0, buf1, layer, *, tn):
    m, d = x.shape
    nf = w_gate.shape[2]
    wspec = pl.BlockSpec((1, d, tn), lambda j: (layer, 0, j))
    cspec = pl.BlockSpec((m, tn), lambda j: (0, j))
    return pl.pallas_call(
        _ffn_up_sample_kernel,
        grid=(nf // tn,),
        in_specs=[pl.BlockSpec((m, d), lambda j: (0, 0)), wspec, wspec,
                  pl.BlockSpec((1, CONV_W, tn), lambda j: (layer, 0, j)),
                  pl.BlockSpec((1, 1, tn), lambda j: (layer, 0, j)),
                  cspec, cspec],
        out_specs=[cspec, cspec, cspec],
        out_shape=[jax.ShapeDtypeStruct((m, nf), BF16),
                   jax.ShapeDtypeStruct((m, nf), F32),
                   jax.ShapeDtypeStruct((m, nf), F32)],
        compiler_params=_cparams("parallel"),
        name="ffn_up_sample",
    )(x, w_gate, w_up, conv_w, conv_b, buf0, buf1)


def _ffn_down_kernel(alpha, g_ref, w_ref, x_ref, lg_ref, lb_ref, o_ref, acc_ref):
    kk = pl.program_id(1)

    @pl.when(kk == 0)
    def _():
        acc_ref[...] = jnp.zeros_like(acc_ref)

    acc_ref[...] += jnp.dot(g_ref[...], w_ref[0], preferred_element_type=F32)

    @pl.when(kk == pl.num_programs(1) - 1)
    def _():
        o_ref[...] = _layer_norm_rows(alpha * x_ref[...] + acc_ref[...], lg_ref[0], lb_ref[0])


def _ffn_down_ln(g, w_down, layer, x, ln_g, ln_b, *, alpha, tm, tk):
    m, nf = g.shape
    d = x.shape[1]
    vec = pl.BlockSpec((1, 1, d), lambda i, kk: (layer, 0, 0))
    return pl.pallas_call(
        functools.partial(_ffn_down_kernel, alpha),
        grid=(m // tm, nf // tk),
        in_specs=[pl.BlockSpec((tm, tk), lambda i, kk: (i, kk)),
                  pl.BlockSpec((1, tk, d), lambda i, kk: (layer, kk, 0)),
                  pl.BlockSpec((tm, d), lambda i, kk: (i, 0)), vec, vec],
        out_specs=pl.BlockSpec((tm, d), lambda i, kk: (i, 0)),
        out_shape=jax.ShapeDtypeStruct((m, d), F32),
        scratch_shapes=[pltpu.VMEM((tm, d), F32)],
        compiler_params=_cparams("parallel", "arbitrary"),
        name="ffn_down_ln",
    )(g, w_down, x, ln_g, ln_b)


def _ple_kernel(x_ref, p_ref, wg_ref, wp_ref, of_ref, ob_ref, wgb_ref):
    @pl.when(pl.program_id(0) == 0)
    def _():
        wgb_ref[...] = wg_ref[0].astype(BF16)

    x = x_ref[...]
    gate = _sigmoid(jnp.dot(x.astype(BF16), wgb_ref[...], preferred_element_type=F32))
    emb = jnp.dot(p_ref[0].astype(BF16), wp_ref[0].astype(BF16), preferred_element_type=F32)
    y = x + gate * emb
    of_ref[...] = y
    ob_ref[...] = y.astype(BF16)


def _ple(x, p, w_ple_gate, w_ple, layer, *, tm):
    m, d = x.shape
    pd = p.shape[2]
    row = lambda width: pl.BlockSpec((tm, width), lambda i: (i, 0))
    return pl.pallas_call(
        _ple_kernel,
        grid=(m // tm,),
        in_specs=[row(d), pl.BlockSpec((1, tm, pd), lambda i: (layer, i, 0)),
                  pl.BlockSpec((1, d, d), lambda i: (layer, 0, 0), pipeline_mode=pl.Buffered(1)),
                  pl.BlockSpec((1, pd, d), lambda i: (layer, 0, 0))],
        out_specs=[row(d), row(d)],
        out_shape=[jax.ShapeDtypeStruct((m, d), F32), jax.ShapeDtypeStruct((m, d), BF16)],
        scratch_shapes=[pltpu.VMEM((d, d), BF16)],
        compiler_params=_cparams("arbitrary"),
        name="ple",
    )(x, p, w_ple_gate, w_ple)


def _sample_select_kernel(q_ref, ks_ref, o_ref):
    nblk, n_heads = ks_ref.shape[1], ks_ref.shape[2]
    prod = ks_ref[0] * q_ref[0][None] * (1.0 / MOBA_BLOCK)
    gate = jnp.sum(prod, axis=-1, keepdims=True)
    blk = lax.broadcasted_iota(jnp.int32, gate.shape, 0).astype(F32)
    for t in range(MOBA_TOPK):
        best = jnp.max(gate, axis=0, keepdims=True)
        idx = jnp.min(jnp.where(gate == best, blk, float(nblk)), axis=0, keepdims=True)
        o_ref[0, t] = jnp.broadcast_to(idx[0], (n_heads, 128)).astype(jnp.int32)
        gate = jnp.where(blk == idx, -jnp.inf, gate)


def _sample_select(q_heads, ksum_l):
    bs, nblk, n_heads, hd = ksum_l.shape
    return pl.pallas_call(
        _sample_select_kernel,
        grid=(bs,),
        in_specs=[pl.BlockSpec((1, n_heads, hd), lambda b: (b, 0, 0)),
                  pl.BlockSpec((1, nblk, n_heads, hd), lambda b: (b, 0, 0, 0))],
        out_specs=pl.BlockSpec((1, MOBA_TOPK, n_heads, 128), lambda b: (b, 0, 0, 0)),
        out_shape=jax.ShapeDtypeStruct((bs, MOBA_TOPK, n_heads, 128), jnp.int32),
        compiler_params=_cparams("parallel"),
        name="sample_select",
    )(q_heads, ksum_l)


def _sample_attn_kernel(layer, n_sel_pages, pid_ref, near_ref, q_ref, kn_ref, vn_ref, ck_hbm, cv_hbm,
                        bn_ref, bf_ref, o_ref, kbuf, vbuf, sem):
    b = pl.program_id(0)
    n_seq = pl.num_programs(0)
    n_heads = o_ref.shape[1]
    ppb, page = bn_ref.shape[1], bn_ref.shape[2]
    scale = HEAD_DIM ** -0.5

    def page_copies(seq, slot):
        copies = []
        for hh in range(n_heads):
            for j in range(n_sel_pages):
                pid = pid_ref[(seq * n_heads + hh) * n_sel_pages + j]
                copies.append(pltpu.make_async_copy(ck_hbm.at[layer, pid, :, hh, :], kbuf.at[slot, hh, j],
                                                    sem.at[slot]))
                copies.append(pltpu.make_async_copy(cv_hbm.at[layer, pid, :, hh, :], vbuf.at[slot, hh, j],
                                                    sem.at[slot]))
        return copies

    slot = lax.rem(b, 2)

    @pl.when(b == 0)
    def _():
        for cp in page_copies(0, 0):
            cp.start()

    @pl.when(b + 1 < n_seq)
    def _():
        for cp in page_copies(b + 1, 1 - slot):
            cp.start()

    for cp in page_copies(b, slot):
        cp.wait()

    for hh in range(n_heads):
        sl = slice(hh * HEAD_DIM, (hh + 1) * HEAD_DIM)
        q = q_ref[0, :, sl]
        qb = jnp.broadcast_to(q, (8, HEAD_DIM)).astype(BF16)
        ks = kbuf[slot, hh].reshape(n_sel_pages * page, HEAD_DIM).astype(BF16)
        vs = vbuf[slot, hh].reshape(n_sel_pages * page, HEAD_DIM).astype(BF16)
        s = lax.dot_general(qb, ks, _NT, preferred_element_type=F32)[0:1, :] * scale
        far_bias = bf_ref[hh, 0:1, 0:1]
        bias = []
        for j in range(n_sel_pages):
            near = near_ref[(b * n_heads + hh) * (n_sel_pages // ppb) + j // ppb]
            bias.append(jnp.where(near == 1, bn_ref[hh, j % ppb:j % ppb + 1, :], far_bias))
        s = s + jnp.concatenate(bias, axis=1)
        kn = kn_ref[0, :, sl]
        s_own = jnp.sum(q.astype(BF16).astype(F32) * kn.astype(BF16).astype(F32), axis=-1, keepdims=True)
        s_own = s_own * scale + bf_ref[hh, 1:2, 0:1]
        m = jnp.maximum(jnp.max(s, axis=-1, keepdims=True), s_own)
        p = jnp.exp(s - m)
        p_own = jnp.exp(s_own - m)
        l = jnp.sum(p, axis=-1, keepdims=True) + p_own
        pv = jnp.dot(jnp.broadcast_to(p, (8, p.shape[1])).astype(BF16), vs, preferred_element_type=F32)[0:1, :]
        acc = pv + p_own.astype(BF16).astype(F32) * vn_ref[0, :, sl].astype(BF16).astype(F32)
        o_ref[0, hh] = (acc / l).astype(o_ref.dtype)


def _sample_attn(hs3, cache_k, cache_v, layer, sel_pages, near, bias_near, bias_far, *, n_sel_pages):
    bs = hs3.shape[0]
    _, _, page, n_heads, hd = cache_k.shape
    width = n_heads * hd
    hspec = lambda g: pl.BlockSpec((1, 1, width), lambda b, pid, nr: (b, 0, g))
    whole = lambda a: pl.BlockSpec(a.shape, lambda b, pid, nr: (0,) * a.ndim)
    buf = pltpu.VMEM((2, n_heads, n_sel_pages, page, hd), cache_k.dtype)
    return pl.pallas_call(
        functools.partial(_sample_attn_kernel, layer, n_sel_pages),
        grid_spec=pltpu.PrefetchScalarGridSpec(
            num_scalar_prefetch=2,
            grid=(bs,),
            in_specs=[hspec(0), hspec(1), hspec(2),
                      pl.BlockSpec(memory_space=pl.ANY), pl.BlockSpec(memory_space=pl.ANY),
                      whole(bias_near), whole(bias_far)],
            out_specs=pl.BlockSpec((1, n_heads, 1, hd), lambda b, pid, nr: (b, 0, 0, 0)),
            scratch_shapes=[buf, buf, pltpu.SemaphoreType.DMA((2,))],
        ),
        out_shape=jax.ShapeDtypeStruct((bs, n_heads, 1, hd), BF16),
        compiler_params=_cparams("arbitrary"),
        name="sample_attn",
    )(sel_pages, near, hs3, hs3, hs3, cache_k, cache_v, bias_near, bias_far)


def _ret_sample_kernel(q_ref, k_ref, v_ref, g_ref, cos_ref, sin_ref, dec_ref, s0_ref, o_ref, s_ref):
    n_heads = s_ref.shape[1]
    scale = HEAD_DIM ** -0.5
    cos = cos_ref[...]
    sin = sin_ref[...]
    rows = lax.broadcasted_iota(jnp.int32, (HEAD_DIM, HEAD_DIM), 0)
    cols = lax.broadcasted_iota(jnp.int32, (HEAD_DIM, HEAD_DIM), 1)
    eye = jnp.where(rows == cols, 1.0, 0.0)
    for hh in range(n_heads):
        sl = slice(hh * HEAD_DIM, (hh + 1) * HEAD_DIM)
        q = _rotate(q_ref[0, :, sl], cos, sin)
        k = _rotate(k_ref[0, :, sl], cos, sin) * scale
        v = v_ref[0, :, sl]
        g = g_ref[0, :, sl]
        dec = dec_ref[hh:hh + 1, :]
        q_col = jnp.sum(eye * q, axis=-1, keepdims=True)
        k_col = jnp.sum(eye * k, axis=-1, keepdims=True)
        s = s0_ref[0, hh]
        qk = jnp.sum(q * k, axis=-1, keepdims=True)
        o = qk * v + jnp.sum(q_col * s, axis=0, keepdims=True) * dec
        s_ref[0, hh] = s * dec + k_col * v
        o_ref[0, :, sl] = _head_norm_gate(o, g).astype(o_ref.dtype)


def _ret_sample(hs, cos, sin, decay, s0, *, n_heads, col0):
    bs = hs.shape[0]
    width = n_heads * HEAD_DIM
    hspec = lambda g: pl.BlockSpec((1, 1, width), lambda b: (b, 0, col0 + g))
    vspec = pl.BlockSpec((1, HEAD_DIM), lambda b: (0, 0))
    sspec = pl.BlockSpec((1, n_heads, HEAD_DIM, HEAD_DIM), lambda b: (b, 0, 0, 0))
    return pl.pallas_call(
        _ret_sample_kernel,
        grid=(bs,),
        in_specs=[hspec(0), hspec(1), hspec(2), hspec(3), vspec, vspec,
                  pl.BlockSpec((n_heads, HEAD_DIM), lambda b: (0, 0)), sspec],
        out_specs=[pl.BlockSpec((1, 1, width), lambda b: (b, 0, 0)), sspec],
        out_shape=[jax.ShapeDtypeStruct((bs, 1, width), BF16),
                   jax.ShapeDtypeStruct((bs, n_heads, HEAD_DIM, HEAD_DIM), F32)],
        compiler_params=_cparams("parallel"),
        name="ret_sample",
    )(hs, hs, hs, hs, cos, sin, decay, s0)


def _t5_bucket(rel):
    n = jnp.maximum(rel, 0)
    max_exact = NUM_BUCKETS // 2
    nf = jnp.maximum(n, max_exact).astype(F32)
    large = max_exact + (jnp.log(nf / max_exact) / math.log(MAX_DISTANCE / max_exact)
                         * (NUM_BUCKETS - max_exact)).astype(jnp.int32)
    return jnp.where(n < max_exact, n, jnp.minimum(large, NUM_BUCKETS - 1))


def _rope_tables(pos):
    half = HEAD_DIM // 2
    inv = ROPE_BASE ** (-jnp.arange(half, dtype=F32) / half)
    ang = pos[:, None] * inv[None, :]
    cos = jnp.cos(ang)
    sin = jnp.sin(ang)
    return jnp.concatenate([cos, cos], -1), jnp.concatenate([-sin, sin], -1)


def _decay_tables(n_heads, chunk):
    lg = jnp.log(1.0 - 2.0 ** (-5.0 - jnp.arange(n_heads, dtype=F32)))
    i = jnp.arange(chunk, dtype=F32)
    diff = i[:, None] - i[None, :]
    dmask = jnp.where(diff >= 0, jnp.exp(lg[:, None, None] * jnp.maximum(diff, 0.0)), 0.0)
    full = lambda col: jnp.broadcast_to(col[..., None], (n_heads, chunk, HEAD_DIM))
    cross = full(jnp.exp(lg[:, None] * (i + 1.0)))
    upd = full(jnp.exp(lg[:, None] * (chunk - 1.0 - i)))
    cdec = jnp.broadcast_to(jnp.exp(lg * chunk)[:, None, None], (n_heads, chunk, HEAD_DIM))
    return dmask, cross, upd, cdec


def kernel(x_prompt, x_sample, cache_k, cache_v, state_ret, state_conv, page_table, p_prompt, p_sample,
           rel_bias, w_in, w_out, ln1_g, ln1_b, w_gate, w_up, conv_w, conv_b, w_down, ln2_g, ln2_b,
           w_ple, w_ple_gate):
    bp, seq, d = x_prompt.shape
    bs, dec_seq, _ = x_sample.shape
    depth, n_pool, page, n_heads, hd = cache_k.shape
    n_pages = page_table.shape[1]
    past_len = n_pages * page
    nf = w_gate.shape[2]
    attn_w = n_heads * hd
    assert hd == HEAD_DIM and dec_seq == 1 and MOBA_BLOCK % page == 0
    assert seq % MOBA_BLOCK == 0 and past_len % MOBA_BLOCK == 0 and past_len // MOBA_BLOCK >= MOBA_TOPK
    assert w_in.shape[2] == 7 * attn_w and d == 2 * attn_w and RET_CHUNK == HEAD_DIM
    assert int(np.floor(np.log(np.float32(MOBA_BLOCK + 1) / 16) / math.log(MAX_DISTANCE / 16) * 16)) >= 15
    alpha = (2 * depth) ** 0.25
    ppb = MOBA_BLOCK // page
    n_past_blocks = past_len // MOBA_BLOCK
    mp = bp * seq

    w_down_b = w_down.astype(BF16)
    vec3 = lambda a: a.reshape(depth, 1, a.shape[-1])
    ln1_g3, ln1_b3, ln2_g3, ln2_b3, conv_b3 = map(vec3, (ln1_g, ln1_b, ln2_g, ln2_b, conv_b))

    def bias_of(rel):
        onehot = (_t5_bucket(rel)[..., None] == jnp.arange(NUM_BUCKETS)).astype(F32)
        return jnp.einsum("...b,bh->h...", onehot, rel_bias.astype(F32), precision=lax.Precision.HIGHEST)

    ii = jnp.arange(MOBA_BLOCK, dtype=jnp.int32)
    far = rel_bias[NUM_BUCKETS - 1].astype(F32)
    bias_diag = bias_of(ii[:, None] - ii[None, :]) * LOG2E
    bias_adj = bias_of(MOBA_BLOCK + ii[:, None] - ii[None, :]) * LOG2E
    bias_far = jnp.broadcast_to((far * LOG2E)[:, None, None], (n_heads, 1, 128))
    bias_near_s = bias_of(MOBA_BLOCK - ii).reshape(n_heads, ppb, page)
    bias_far_s = jnp.stack([jnp.broadcast_to(far[:, None], (n_heads, 128)),
                            jnp.broadcast_to(rel_bias[0].astype(F32)[:, None], (n_heads, 128))], axis=1)

    cos_p, sin_p = _rope_tables(jnp.arange(seq, dtype=F32))
    cos_s, sin_s = _rope_tables(jnp.full((1,), float(past_len), F32))
    tabs = _decay_tables(n_heads, RET_CHUNK)
    decay_s = jnp.broadcast_to(
        (1.0 - 2.0 ** (-5.0 - jnp.arange(n_heads, dtype=F32)))[:, None], (n_heads, HEAD_DIM))
    decay_s = jnp.exp(jnp.log(decay_s))

    ck = cache_k.reshape(depth, n_pool, page * n_heads, hd)
    page_table_flat = page_table.reshape(-1)

    xp_f = x_prompt.reshape(mp, d)
    xp_b = xp_f.astype(BF16)
    xs_f = x_sample.reshape(bs, d)
    xs_b = xs_f.astype(BF16)
    s0_prompt = jnp.zeros((bp, n_heads, hd, hd), F32)
    conv0_prompt = jnp.zeros((bp, CONV_W - 1, nf), F32)

    pp3 = p_prompt.reshape(depth, mp, -1)
    ps3 = p_sample.reshape(depth, bs, -1)
    outs = {k: [] for k in ("ks", "vs", "rp", "rs", "cp", "cs")}
    kp_all = vp_all = None
    for l in range(depth):
        qp = _matmul(xp_b, w_in, l, col0=0, n=attn_w, tm=1024, tn=1024)
        tm_kv = 1024 if l else 512
        kp_all = _matmul_stacked(xp_b, w_in, l, kp_all, col0=attn_w, n=attn_w, tm=tm_kv, tn=1024)
        vp_all = _matmul_stacked(xp_b, w_in, l, vp_all, col0=2 * attn_w, n=attn_w, tm=tm_kv, tn=1024)
        hr = _matmul(xp_b, w_in, l, col0=3 * attn_w, n=4 * attn_w, tm=1024, tn=1024)
        attn = _moba_prompt(qp, kp_all, vp_all, l, bias_diag, bias_adj, bias_far,
                            batch=bp, seq=seq, n_heads=n_heads)
        ret, s_new = _ret_prompt(hr, cos_p, sin_p, tabs, s0_prompt, batch=bp, seq=seq, n_heads=n_heads, col0=0)
        outs["rp"].append(s_new)
        x1_f, x1_b = _outproj_ln(attn, ret, w_out, l, xp_f, ln1_g3, ln1_b3, alpha=alpha, tm=512)
        gact, conv_new, ksum_l = _ffn_up_prompt(x1_b, w_gate, w_up, conv_w, conv_b3, conv0_prompt, l,
                                                page_table_flat, ck, batch=bp, seq=seq, tm=1024, tn=512,
                                                pages_per_block=ppb, n_heads=n_heads)
        ksum_l = ksum_l.reshape(bs, n_past_blocks, n_heads, hd)
        outs["cp"].append(conv_new)
        x2 = _ffn_down_ln(gact, w_down_b, l, x1_f, ln2_g3, ln2_b3, alpha=alpha, tm=512, tk=nf // 2)
        xp_f, xp_b = _ple(x2, pp3, w_ple_gate, w_ple, l, tm=512)

        hs = _matmul(xs_b, w_in, l, col0=0, n=w_in.shape[2], tm=bs, tn=1024)
        outs["ks"].append(hs[:, attn_w:2 * attn_w].reshape(bs, 1, n_heads, hd))
        outs["vs"].append(hs[:, 2 * attn_w:3 * attn_w].reshape(bs, 1, n_heads, hd))
        hs3 = hs.reshape(bs, 1, -1)
        picks = _sample_select(hs[:, :attn_w].reshape(bs, n_heads, hd), ksum_l)
        blocks = jnp.transpose(picks[..., 0], (0, 2, 1))
        sel_pos = blocks[..., None] * ppb + jnp.arange(ppb, dtype=jnp.int32)
        sel_pages = page_table[jnp.arange(bs)[:, None, None, None], sel_pos]
        near = (blocks == n_past_blocks - 1).astype(jnp.int32)
        attn_s = _sample_attn(hs3, cache_k, cache_v, l, sel_pages.reshape(-1), near.reshape(-1),
                              bias_near_s, bias_far_s, n_sel_pages=MOBA_TOPK * ppb)
        ret_s, s_new_s = _ret_sample(hs3, cos_s, sin_s, decay_s, state_ret[l], n_heads=n_heads, col0=3)
        outs["rs"].append(s_new_s)
        x1s_f, x1s_b = _outproj_ln(attn_s.reshape(bs, attn_w), ret_s.reshape(bs, attn_w), w_out, l, xs_f,
                                   ln1_g3, ln1_b3, alpha=alpha, tm=bs)
        gact_s, c0, c1 = _ffn_up_sample(x1s_b, w_gate, w_up, conv_w, conv_b3,
                                        state_conv[l, :, 0], state_conv[l, :, 1], l, tn=512)
        outs["cs"].append(jnp.stack([c0, c1], axis=1))
        x2s = _ffn_down_ln(gact_s, w_down_b, l, x1s_f, ln2_g3, ln2_b3, alpha=alpha, tm=bs, tk=nf // 4)
        xs_f, xs_b = _ple(x2s, ps3, w_ple_gate, w_ple, l, tm=bs)

    stack = lambda key: jnp.stack(outs[key])
    return (xp_f.reshape(bp, seq, d), xs_f.reshape(bs, 1, d),
            kp_all.reshape(depth, bp, seq, n_heads, hd), vp_all.reshape(depth, bp, seq, n_heads, hd),
            stack("ks"), stack("vs"), stack("rp"), stack("rs"), stack("cp"), stack("cs"))
```

```python
import functools
import math

import jax
import jax.numpy as jnp
import numpy as np
from jax import lax
from jax.experimental import pallas as pl
from jax.experimental.pallas import tpu as pltpu

HEAD_DIM = 128
MOBA_BLOCK = 256
MOBA_TOPK = 3
RET_CHUNK = 128
ROPE_BASE = 10000.0
NUM_BUCKETS = 32
MAX_DISTANCE = 128
CONV_W = 3
SAMPLE_ROWS_PAD = 16
LN_EPS = 1e-5

VMEM_LIMIT_BYTES = 56 * 1024 * 1024
NEG_BIG = -1e30
LOG2E = math.log2(math.e)

BF16 = jnp.bfloat16
F32 = jnp.float32

_NT = (((1,), (1,)), ((), ()))
_TN = (((0,), (0,)), ((), ()))


def _cparams(*sem):
    return pltpu.CompilerParams(dimension_semantics=sem, vmem_limit_bytes=VMEM_LIMIT_BYTES)


def _sigmoid(x):
    return 1.0 / (1.0 + jnp.exp(-x))


def _layer_norm_rows(y, g, b):
    mu = jnp.mean(y, axis=-1, keepdims=True)
    d = y - mu
    var = jnp.mean(d * d, axis=-1, keepdims=True)
    return d * lax.rsqrt(var + LN_EPS) * g + b


def _split_bf16(a):
    hi = a.astype(BF16)
    lo = (a - hi.astype(F32)).astype(BF16)
    return hi, lo


def _dot_nt_precise(a, b):
    ah, al = _split_bf16(a)
    bh, bl = _split_bf16(b)
    dot = functools.partial(lax.dot_general, dimension_numbers=_NT, preferred_element_type=F32)
    return dot(ah, bh) + (dot(ah, bl) + dot(al, bh))


def _mm_kernel(slab, x_ref, w_ref, xs_ref, *rest):
    o_ref, os_ref, wb_ref = rest[-3:]
    i = pl.program_id(1)

    @pl.when(i == 0)
    def _():
        wb_ref[...] = w_ref[0].astype(BF16)

    res = jnp.dot(x_ref[...], wb_ref[...], preferred_element_type=F32)
    if slab is None:
        o_ref[...] = res
    else:
        for other in range(o_ref.shape[0]):
            if other != slab:
                o_ref[other] = jnp.zeros(o_ref.shape[1:], o_ref.dtype)
        o_ref[slab] = res

    @pl.when(i == pl.num_programs(1) - 1)
    def _():
        os_ref[...] = jnp.dot(xs_ref[...], wb_ref[...], preferred_element_type=F32)


def _matmul(x, xs, w, layer, *, col0, n, tm, tn, stack=None, stacked=False):
    m, k = x.shape
    ms = xs.shape[0]
    depth = w.shape[0]
    assert col0 % tn == 0 and n % tn == 0
    in_specs = [pl.BlockSpec((tm, k), lambda j, i: (i, 0)),
                pl.BlockSpec((1, k, tn), lambda j, i: (layer, 0, col0 // tn + j)),
                pl.BlockSpec((ms, k), lambda j, i: (0, 0))]
    operands, aliases, slab = [x, w, xs], {}, None
    if not stacked:
        out_spec = pl.BlockSpec((tm, tn), lambda j, i: (i, j))
        out_shape = jax.ShapeDtypeStruct((m, n), F32)
    else:
        out_shape = jax.ShapeDtypeStruct((depth, m, n), F32)
        if stack is None:
            slab = layer
            out_spec = pl.BlockSpec((depth, tm, tn), lambda j, i: (0, i, j))
        else:
            in_specs.append(pl.BlockSpec(memory_space=pl.ANY))
            operands, aliases, slab = [x, w, xs, stack], {3: 0}, 0
            out_spec = pl.BlockSpec((1, tm, tn), lambda j, i: (layer, i, j))
    return pl.pallas_call(
        functools.partial(_mm_kernel, slab),
        grid=(n // tn, m // tm),
        in_specs=in_specs,
        out_specs=[out_spec, pl.BlockSpec((ms, tn), lambda j, i: (0, j))],
        out_shape=[out_shape, jax.ShapeDtypeStruct((ms, n), F32)],
        scratch_shapes=[pltpu.VMEM((k, tn), BF16)],
        input_output_aliases=aliases,
        compiler_params=_cparams("parallel", "arbitrary"),
        name="proj_in",
    )(*operands)


def _moba_select(cc, qf, km_ref):
    tq = qf.shape[0]
    nbp = km_ref.shape[0]
    gate = _dot_nt_precise(km_ref[...], qf)
    blk = lax.broadcasted_iota(jnp.int32, (nbp, tq), 0)
    eligible = blk < cc
    gate = jnp.where(eligible, gate, -jnp.inf)
    rank = jnp.zeros((nbp, tq), F32)
    for mth in range(cc):
        gm = gate[mth:mth + 1, :]
        beats = jnp.logical_or(gm > gate, jnp.logical_and(gm == gate, mth < blk))
        rank = rank + jnp.where(beats, 1.0, 0.0)
    sel_t = jnp.where(jnp.logical_and(eligible, rank < MOBA_TOPK), 1.0, 0.0)
    sel_pad = jnp.concatenate([sel_t, jnp.zeros((128 - nbp, tq), F32)], axis=0).astype(BF16)
    rows = lax.broadcasted_iota(jnp.int32, (tq, tq), 0)
    cols = lax.broadcasted_iota(jnp.int32, (tq, tq), 1)
    eye = jnp.where(rows == cols, 1.0, 0.0).astype(BF16)
    return lax.dot_general(eye, sel_pad, _NT, preferred_element_type=F32)


def _moba_tile(cc, q_ref, bd_ref, ba_ref, bf_ref, o_ref, kb_ref, vb_ref, km_ref):
    tq = q_ref.shape[0]
    scale2 = HEAD_DIM ** -0.5 * LOG2E
    nk = (cc + 1) * MOBA_BLOCK
    rows = lax.broadcasted_iota(jnp.int32, (tq, MOBA_BLOCK), 0)
    cols = lax.broadcasted_iota(jnp.int32, (tq, MOBA_BLOCK), 1)
    for hp in range(q_ref.shape[1] // HEAD_DIM):
        sl = slice(hp * HEAD_DIM, (hp + 1) * HEAD_DIM)
        qf = q_ref[:, sl]
        s = lax.dot_general(qf.astype(BF16), kb_ref[0:nk, sl], _NT, preferred_element_type=F32) * scale2
        sel = _moba_select(cc, qf, km_ref.at[hp]) if cc > MOBA_TOPK else None
        far_bias = bf_ref[hp, 0:1, 0:1]
        pieces = []
        for n in range(cc + 1):
            sn = s[:, n * MOBA_BLOCK:(n + 1) * MOBA_BLOCK]
            if n == cc:
                sn = jnp.where(cols <= rows, sn + bd_ref[hp], NEG_BIG)
            elif n == cc - 1:
                sn = sn + ba_ref[hp]
            else:
                sn = sn + far_bias
            if n < cc and sel is not None:
                sn = jnp.where(sel[:, n:n + 1] > 0.5, sn, NEG_BIG)
            pieces.append(sn)
        s = jnp.concatenate(pieces, axis=1) if cc else pieces[0]
        m = jnp.max(s, axis=-1, keepdims=True)
        p = jnp.exp2(s - m)
        l = jnp.sum(p, axis=-1, keepdims=True)
        acc = jnp.dot(p.astype(BF16), vb_ref[0:nk, sl], preferred_element_type=F32)
        o_ref[:, sl] = (acc / l).astype(o_ref.dtype)


def _moba_prompt_kernel(q_ref, k_ref, v_ref, bd_ref, ba_ref, bf_ref, o_ref, kb_ref, vb_ref, km_ref):
    c = pl.program_id(2)
    nb = k_ref.shape[1] // MOBA_BLOCK

    @pl.when(c == 0)
    def _():
        kb_ref[...] = k_ref[0].astype(BF16)
        vb_ref[...] = v_ref[0].astype(BF16)
        km_ref[...] = jnp.zeros_like(km_ref)
        for hp in range(km_ref.shape[0]):
            sl = slice(hp * HEAD_DIM, (hp + 1) * HEAD_DIM)
            for n in range(nb):
                km_ref[hp, n:n + 1, :] = jnp.mean(k_ref[0, n * MOBA_BLOCK:(n + 1) * MOBA_BLOCK, sl],
                                                  axis=0, keepdims=True)

    for cc in range(nb):
        pl.when(c == cc)(functools.partial(_moba_tile, cc, q_ref, bd_ref, ba_ref, bf_ref, o_ref,
                                           kb_ref, vb_ref, km_ref))


MOBA_HEADS_PER_STEP = 2


def _moba_prompt(q, k, v, layer, bias_diag, bias_adj, bias_far, *, batch, seq, n_heads):
    nq = seq // MOBA_BLOCK
    nbias = bias_far.shape[1]
    hps = MOBA_HEADS_PER_STEP
    width = hps * HEAD_DIM
    kvspec = pl.BlockSpec((1, seq, width), lambda b, hh, qi: (layer, b, hh))
    bspec = pl.BlockSpec((hps, MOBA_BLOCK, MOBA_BLOCK), lambda b, hh, qi: (hh, 0, 0))
    return pl.pallas_call(
        _moba_prompt_kernel,
        grid=(batch, n_heads // hps, nq),
        in_specs=[
            pl.BlockSpec((MOBA_BLOCK, width), lambda b, hh, qi: (b * nq + qi, hh)),
            kvspec, kvspec, bspec, bspec,
            pl.BlockSpec((hps, nbias, 128), lambda b, hh, qi: (hh, 0, 0)),
        ],
        out_specs=pl.BlockSpec((MOBA_BLOCK, width), lambda b, hh, qi: (b * nq + qi, hh)),
        out_shape=jax.ShapeDtypeStruct((batch * seq, n_heads * HEAD_DIM), BF16),
        scratch_shapes=[
            pltpu.VMEM((seq, width), BF16),
            pltpu.VMEM((seq, width), BF16),
            pltpu.VMEM((hps, max(16, seq // MOBA_BLOCK), HEAD_DIM), F32),
        ],
        compiler_params=_cparams("parallel", "parallel", "arbitrary"),
        name="moba_prompt",
    )(q, k, v, bias_diag, bias_adj, bias_far)


def _rotate(x, cos, sin_signed):
    return x * cos + pltpu.roll(x, HEAD_DIM // 2, 1) * sin_signed


def _head_norm_gate(o, g):
    mu = jnp.mean(o, axis=-1, keepdims=True)
    d = o - mu
    var = jnp.mean(d * d, axis=-1, keepdims=True)
    return g * _sigmoid(g) * (d * lax.rsqrt(var + LN_EPS))


def _ret_prompt_kernel(q_ref, k_ref, v_ref, g_ref, cos_ref, sin_ref, dm_ref, cr_ref, up_ref, cd_ref,
                       s0_ref, o_ref, s_ref):
    n_heads = s_ref.shape[1]
    scale = HEAD_DIM ** -0.5

    @pl.when(pl.program_id(1) == 0)
    def _():
        s_ref[...] = s0_ref[...]

    cos = cos_ref[...]
    sin = sin_ref[...]
    for hh in range(n_heads):
        sl = slice(hh * HEAD_DIM, (hh + 1) * HEAD_DIM)
        q = _rotate(q_ref[:, sl], cos, sin)
        k = _rotate(k_ref[:, sl], cos, sin) * scale
        qb = q.astype(BF16)
        kb = k.astype(BF16)
        vb = v_ref[:, sl].astype(BF16)
        s = s_ref[0, hh]
        sc = lax.dot_general(qb, kb, _NT, preferred_element_type=F32) * dm_ref[hh]
        o = jnp.dot(sc.astype(BF16), vb, preferred_element_type=F32)
        o = o + jnp.dot(qb, s.astype(BF16), preferred_element_type=F32) * cr_ref[hh]
        ku = (k * up_ref[hh]).astype(BF16)
        s_ref[0, hh] = s * cd_ref[hh] + lax.dot_general(ku, vb, _TN, preferred_element_type=F32)
        o_ref[:, sl] = _head_norm_gate(o, g_ref[:, sl]).astype(o_ref.dtype)


def _ret_prompt(h, cos, sin, tabs, s0, *, batch, seq, n_heads, col0):
    nc = seq // RET_CHUNK
    width = n_heads * HEAD_DIM
    hspec = lambda g: pl.BlockSpec((RET_CHUNK, width), lambda b, c: (b * nc + c, col0 + g))
    tspec = pl.BlockSpec((n_heads, RET_CHUNK, HEAD_DIM), lambda b, c: (0, 0, 0))
    sspec = pl.BlockSpec((1, n_heads, HEAD_DIM, HEAD_DIM), lambda b, c: (b, 0, 0, 0))
    return pl.pallas_call(
        _ret_prompt_kernel,
        grid=(batch, nc),
        in_specs=[hspec(0), hspec(1), hspec(2), hspec(3),
                  pl.BlockSpec((RET_CHUNK, HEAD_DIM), lambda b, c: (c, 0)),
                  pl.BlockSpec((RET_CHUNK, HEAD_DIM), lambda b, c: (c, 0)),
                  tspec, tspec, tspec, tspec, sspec],
        out_specs=[pl.BlockSpec((RET_CHUNK, width), lambda b, c: (b * nc + c, 0)), sspec],
        out_shape=[jax.ShapeDtypeStruct((batch * seq, width), BF16),
                   jax.ShapeDtypeStruct((batch, n_heads, HEAD_DIM, HEAD_DIM), F32)],
        compiler_params=_cparams("parallel", "arbitrary"),
        name="ret_prompt",
    )(h, h, h, h, cos, sin, *tabs, s0)


def _outproj_kernel(alpha, a_ref, r_ref, wa_ref, wr_ref, x_ref, g_ref, b_ref, of_ref, ob_ref, wb_ref):
    @pl.when(pl.program_id(0) == 0)
    def _():
        wb_ref[0] = wa_ref[0].astype(BF16)
        wb_ref[1] = wr_ref[0].astype(BF16)

    mix = jnp.dot(a_ref[...], wb_ref[0], preferred_element_type=F32)
    mix = mix + jnp.dot(r_ref[...], wb_ref[1], preferred_element_type=F32)
    y = _layer_norm_rows(alpha * x_ref[...] + mix, g_ref[0], b_ref[0])
    of_ref[...] = y
    ob_ref[...] = y.astype(BF16)


def _outproj_ln(attn, ret, w_out, layer, x, g, b, *, alpha, tm):
    m, ka = attn.shape
    d = x.shape[1]
    row = lambda width: pl.BlockSpec((tm, width), lambda i: (i, 0))
    vec = pl.BlockSpec((1, 1, d), lambda i: (layer, 0, 0))
    return pl.pallas_call(
        functools.partial(_outproj_kernel, alpha),
        grid=(m // tm,),
        in_specs=[row(ka), row(ka),
                  pl.BlockSpec((1, ka, d), lambda i: (layer, 0, 0), pipeline_mode=pl.Buffered(1)),
                  pl.BlockSpec((1, ka, d), lambda i: (layer, 1, 0), pipeline_mode=pl.Buffered(1)),
                  row(d), vec, vec],
        out_specs=[row(d), row(d)],
        out_shape=[jax.ShapeDtypeStruct((m, d), F32), jax.ShapeDtypeStruct((m, d), BF16)],
        scratch_shapes=[pltpu.VMEM((2, ka, d), BF16)],
        compiler_params=_cparams("arbitrary"),
        name="outproj_ln",
    )(attn, ret, w_out, w_out, x, g, b)


KSUM_PAGES_PER_STEP = 16


def _ffn_up_prompt_kernel(tiles_per_seq, n_pages, ids_ref, x_ref, wg_ref, wu_ref, cw_ref, cb_ref, buf_ref,
                          *rest):
    del ids_ref
    page_refs = rest[:n_pages]
    o_ref, cn_ref, ks_ref, carry_ref, wgb_ref, wub_ref = rest[n_pages:]
    i = pl.program_id(1)
    tm = x_ref.shape[0]

    @pl.when(i == 0)
    def _():
        wgb_ref[...] = wg_ref[0].astype(BF16)
        wub_ref[...] = wu_ref[0].astype(BF16)

    x = x_ref[...]
    a = jnp.dot(x, wgb_ref[...], preferred_element_type=F32)
    u = jnp.dot(x, wub_ref[...], preferred_element_type=F32)
    first = (i % tiles_per_seq) == 0
    prev = jnp.where(first, buf_ref[0], carry_ref[6:8, :])
    row = lax.broadcasted_iota(jnp.int32, a.shape, 0)
    a1 = jnp.where(row == 0, prev[1:2, :], pltpu.roll(a, 1, 0))
    a2 = jnp.where(row == 0, prev[0:1, :], jnp.where(row == 1, prev[1:2, :], pltpu.roll(a, 2, 0)))
    cw = cw_ref[0]
    conv = cb_ref[0] + cw[0:1, :] * a2 + cw[1:2, :] * a1 + cw[2:3, :] * a
    o_ref[...] = (conv * _sigmoid(conv) * u).astype(o_ref.dtype)
    carry_ref[...] = a[tm - 8:tm, :]

    @pl.when((i % tiles_per_seq) == tiles_per_seq - 1)
    def _():
        cn_ref[0] = a[tm - (CONV_W - 1):tm, :]

    n_heads = ks_ref.shape[1]
    ppb = n_pages // ks_ref.shape[0]
    for r, p_ref in enumerate(page_refs):
        pg = p_ref[0, 0]
        part = jnp.sum(pg.reshape(pg.shape[0] // n_heads, n_heads, pg.shape[1]), axis=0)
        if r % ppb == 0:
            ks_ref[r // ppb] = part
        else:
            ks_ref[r // ppb] += part


def _ffn_up_prompt(x, w_gate, w_up, conv_w, conv_b, conv_buf, layer, page_table_flat, cache_k_rows, *,
                   batch, seq, tm, tn, pages_per_block, n_heads):
    m, d = x.shape
    nf = w_gate.shape[2]
    tps = seq // tm
    n_i = m // tm
    _, _, rows, hd = cache_k_rows.shape
    pps = KSUM_PAGES_PER_STEP
    n_pages = page_table_flat.shape[0]
    n_groups = n_pages // pps
    bps = pps // pages_per_block
    assert n_pages % pps == 0 and pps % pages_per_block == 0 and n_groups <= (nf // tn) * n_i
    group = lambda j, i: jnp.minimum(j * n_i + i, n_groups - 1)
    page_spec = lambda r: pl.BlockSpec(
        (1, 1, rows, hd), lambda j, i, ids: (layer, ids[pps * group(j, i) + r], 0, 0))
    wspec = pl.BlockSpec((1, d, tn), lambda j, i, ids: (layer, 0, j))
    return pl.pallas_call(
        functools.partial(_ffn_up_prompt_kernel, tps, pps),
        grid_spec=pltpu.PrefetchScalarGridSpec(
            num_scalar_prefetch=1,
            grid=(nf // tn, n_i),
            in_specs=[pl.BlockSpec((tm, d), lambda j, i, ids: (i, 0)), wspec, wspec,
                      pl.BlockSpec((1, CONV_W, tn), lambda j, i, ids: (layer, 0, j)),
                      pl.BlockSpec((1, 1, tn), lambda j, i, ids: (layer, 0, j)),
                      pl.BlockSpec((1, CONV_W - 1, tn), lambda j, i, ids: (i // tps, 0, j))]
            + [page_spec(r) for r in range(pps)],
            out_specs=[pl.BlockSpec((tm, tn), lambda j, i, ids: (i, j)),
                       pl.BlockSpec((1, CONV_W - 1, tn), lambda j, i, ids: (i // tps, 0, j)),
                       pl.BlockSpec((bps, n_heads, hd), lambda j, i, ids: (group(j, i), 0, 0))],
            scratch_shapes=[pltpu.VMEM((8, tn), F32), pltpu.VMEM((d, tn), BF16), pltpu.VMEM((d, tn), BF16)],
        ),
        out_shape=[jax.ShapeDtypeStruct((m, nf), BF16),
                   jax.ShapeDtypeStruct((batch, CONV_W - 1, nf), F32),
                   jax.ShapeDtypeStruct((n_pages // pages_per_block, n_heads, hd), F32)],
        compiler_params=_cparams("arbitrary", "arbitrary"),
        name="ffn_up_prompt",
    )(page_table_flat, x, w_gate, w_up, conv_w, conv_b, conv_buf, *([cache_k_rows] * pps))


def _ffn_up_sample_kernel(x_ref, wg_ref, wu_ref, cw_ref, cb_ref, b0_ref, b1_ref, o_ref, n0_ref, n1_ref):
    x = x_ref[...]
    a = jnp.dot(x, wg_ref[0].astype(BF16), preferred_element_type=F32)
    u = jnp.dot(x, wu_ref[0].astype(BF16), preferred_element_type=F32)
    cw = cw_ref[0]
    conv = cb_ref[0] + cw[0:1, :] * b0_ref[...] + cw[1:2, :] * b1_ref[...] + cw[2:3, :] * a
    o_ref[...] = (conv * _sigmoid(conv) * u).astype(o_ref.dtype)
    n0_ref[...] = b1_ref[...]
    n1_ref[...] = a


def _ffn_up_sample(x, w_gate, w_up, conv_w, conv_b, buf0, buf1, layer, *, tn):
    m, d = x.shape
    nf = w_gate.shape[2]
    wspec = pl.BlockSpec((1, d, tn), lambda j: (layer, 0, j))
    cspec = pl.BlockSpec((m, tn), lambda j: (0, j))
    return pl.pallas_call(
        _ffn_up_sample_kernel,
        grid=(nf // tn,),
        in_specs=[pl.BlockSpec((m, d), lambda j: (0, 0)), wspec, wspec,
                  pl.BlockSpec((1, CONV_W, tn), lambda j: (layer, 0, j)),
                  pl.BlockSpec((1, 1, tn), lambda j: (layer, 0, j)),
                  cspec, cspec],
        out_specs=[cspec, cspec, cspec],
        out_shape=[jax.ShapeDtypeStruct((m, nf), BF16),
                   jax.ShapeDtypeStruct((m, nf), F32),
                   jax.ShapeDtypeStruct((m, nf), F32)],
        compiler_params=_cparams("parallel"),
        name="ffn_up_sample",
    )(x, w_gate, w_up, conv_w, conv_b, buf0, buf1)


def _ffn_down_kernel(alpha, g_ref, w_ref, x_ref, lg_ref, lb_ref, o_ref, acc_ref):
    kk = pl.program_id(1)

    @pl.when(kk == 0)
    def _():
        acc_ref[...] = jnp.zeros_like(acc_ref)

    acc_ref[...] += jnp.dot(g_ref[...], w_ref[0], preferred_element_type=F32)

    @pl.when(kk == pl.num_programs(1) - 1)
    def _():
        o_ref[...] = _layer_norm_rows(alpha * x_ref[...] + acc_ref[...], lg_ref[0], lb_ref[0])


def _ffn_down_ln(g, w_down, layer, x, ln_g, ln_b, *, alpha, tm, tk):
    m, nf = g.shape
    d = x.shape[1]
    vec = pl.BlockSpec((1, 1, d), lambda i, kk: (layer, 0, 0))
    return pl.pallas_call(
        functools.partial(_ffn_down_kernel, alpha),
        grid=(m // tm, nf // tk),
        in_specs=[pl.BlockSpec((tm, tk), lambda i, kk: (i, kk)),
                  pl.BlockSpec((1, tk, d), lambda i, kk: (layer, kk, 0)),
                  pl.BlockSpec((tm, d), lambda i, kk: (i, 0)), vec, vec],
        out_specs=pl.BlockSpec((tm, d), lambda i, kk: (i, 0)),
        out_shape=jax.ShapeDtypeStruct((m, d), F32),
        scratch_shapes=[pltpu.VMEM((tm, d), F32)],
        compiler_params=_cparams("parallel", "arbitrary"),
        name="ffn_down_ln",
    )(g, w_down, x, ln_g, ln_b)


def _ple_kernel(x_ref, p_ref, wg_ref, wp_ref, of_ref, ob_ref, wgb_ref):
    @pl.when(pl.program_id(0) == 0)
    def _():
        wgb_ref[...] = wg_ref[0].astype(BF16)

    x = x_ref[...]
    gate = _sigmoid(jnp.dot(x.astype(BF16), wgb_ref[...], preferred_element_type=F32))
    emb = jnp.dot(p_ref[0].astype(BF16), wp_ref[0].astype(BF16), preferred_element_type=F32)
    y = x + gate * emb
    of_ref[...] = y
    ob_ref[...] = y.astype(BF16)


def _ple(x, p, w_ple_gate, w_ple, layer, *, tm):
    m, d = x.shape
    pd = p.shape[2]
    row = lambda width: pl.BlockSpec((tm, width), lambda i: (i, 0))
    return pl.pallas_call(
        _ple_kernel,
        grid=(m // tm,),
        in_specs=[row(d), pl.BlockSpec((1, tm, pd), lambda i: (layer, i, 0)),
                  pl.BlockSpec((1, d, d), lambda i: (layer, 0, 0), pipeline_mode=pl.Buffered(1)),
                  pl.BlockSpec((1, pd, d), lambda i: (layer, 0, 0))],
        out_specs=[row(d), row(d)],
        out_shape=[jax.ShapeDtypeStruct((m, d), F32), jax.ShapeDtypeStruct((m, d), BF16)],
        scratch_shapes=[pltpu.VMEM((d, d), BF16)],
        compiler_params=_cparams("arbitrary"),
        name="ple",
    )(x, p, w_ple_gate, w_ple)


def _sample_select_kernel(q_ref, ks_ref, o_ref):
    nblk, n_heads = ks_ref.shape[1], ks_ref.shape[2]
    prod = ks_ref[0] * q_ref[0][None] * (1.0 / MOBA_BLOCK)
    gate = jnp.sum(prod, axis=-1, keepdims=True)
    blk = lax.broadcasted_iota(jnp.int32, gate.shape, 0).astype(F32)
    for t in range(MOBA_TOPK):
        best = jnp.max(gate, axis=0, keepdims=True)
        idx = jnp.min(jnp.where(gate == best, blk, float(nblk)), axis=0, keepdims=True)
        o_ref[0, t] = jnp.broadcast_to(idx[0], (n_heads, 128)).astype(jnp.int32)
        gate = jnp.where(blk == idx, -jnp.inf, gate)


def _sample_select(q_heads, ksum_l):
    bs, nblk, n_heads, hd = ksum_l.shape
    return pl.pallas_call(
        _sample_select_kernel,
        grid=(bs,),
        in_specs=[pl.BlockSpec((1, n_heads, hd), lambda b: (b, 0, 0)),
                  pl.BlockSpec((1, nblk, n_heads, hd), lambda b: (b, 0, 0, 0))],
        out_specs=pl.BlockSpec((1, MOBA_TOPK, n_heads, 128), lambda b: (b, 0, 0, 0)),
        out_shape=jax.ShapeDtypeStruct((bs, MOBA_TOPK, n_heads, 128), jnp.int32),
        compiler_params=_cparams("parallel"),
        name="sample_select",
    )(q_heads, ksum_l)


def _sample_attn_kernel(layer, n_sel_pages, pid_ref, near_ref, q_ref, kn_ref, vn_ref, ck_hbm, cv_hbm,
                        bn_ref, bf_ref, o_ref, kbuf, vbuf, sem):
    b = pl.program_id(0)
    n_seq = pl.num_programs(0)
    n_heads = o_ref.shape[1]
    ppb, page = bn_ref.shape[1], bn_ref.shape[2]
    scale = HEAD_DIM ** -0.5

    def page_copies(seq, slot):
        copies = []
        for hh in range(n_heads):
            for j in range(n_sel_pages):
                pid = pid_ref[(seq * n_heads + hh) * n_sel_pages + j]
                copies.append(pltpu.make_async_copy(ck_hbm.at[layer, pid, :, hh, :], kbuf.at[slot, hh, j],
                                                    sem.at[slot]))
                copies.append(pltpu.make_async_copy(cv_hbm.at[layer, pid, :, hh, :], vbuf.at[slot, hh, j],
                                                    sem.at[slot]))
        return copies

    slot = lax.rem(b, 2)

    @pl.when(b == 0)
    def _():
        for cp in page_copies(0, 0):
            cp.start()

    @pl.when(b + 1 < n_seq)
    def _():
        for cp in page_copies(b + 1, 1 - slot):
            cp.start()

    for cp in page_copies(b, slot):
        cp.wait()

    for hh in range(n_heads):
        sl = slice(hh * HEAD_DIM, (hh + 1) * HEAD_DIM)
        q = q_ref[0, :, sl]
        qb = jnp.broadcast_to(q, (8, HEAD_DIM)).astype(BF16)
        ks = kbuf[slot, hh].reshape(n_sel_pages * page, HEAD_DIM).astype(BF16)
        vs = vbuf[slot, hh].reshape(n_sel_pages * page, HEAD_DIM).astype(BF16)
        s = lax.dot_general(qb, ks, _NT, preferred_element_type=F32)[0:1, :] * scale
        far_bias = bf_ref[hh, 0:1, 0:1]
        bias = []
        for j in range(n_sel_pages):
            near = near_ref[(b * n_heads + hh) * (n_sel_pages // ppb) + j // ppb]
            bias.append(jnp.where(near == 1, bn_ref[hh, j % ppb:j % ppb + 1, :], far_bias))
        s = s + jnp.concatenate(bias, axis=1)
        kn = kn_ref[0, :, sl]
        s_own = jnp.sum(q.astype(BF16).astype(F32) * kn.astype(BF16).astype(F32), axis=-1, keepdims=True)
        s_own = s_own * scale + bf_ref[hh, 1:2, 0:1]
        m = jnp.maximum(jnp.max(s, axis=-1, keepdims=True), s_own)
        p = jnp.exp(s - m)
        p_own = jnp.exp(s_own - m)
        l = jnp.sum(p, axis=-1, keepdims=True) + p_own
        pv = jnp.dot(jnp.broadcast_to(p, (8, p.shape[1])).astype(BF16), vs, preferred_element_type=F32)[0:1, :]
        acc = pv + p_own.astype(BF16).astype(F32) * vn_ref[0, :, sl].astype(BF16).astype(F32)
        o_ref[0, hh] = (acc / l).astype(o_ref.dtype)


def _sample_attn(hs3, cache_k, cache_v, layer, sel_pages, near, bias_near, bias_far, *, n_sel_pages):
    bs = hs3.shape[0]
    _, _, page, n_heads, hd = cache_k.shape
    width = n_heads * hd
    hspec = lambda g: pl.BlockSpec((1, 1, width), lambda b, pid, nr: (b, 0, g))
    whole = lambda a: pl.BlockSpec(a.shape, lambda b, pid, nr: (0,) * a.ndim)
    buf = pltpu.VMEM((2, n_heads, n_sel_pages, page, hd), cache_k.dtype)
    return pl.pallas_call(
        functools.partial(_sample_attn_kernel, layer, n_sel_pages),
        grid_spec=pltpu.PrefetchScalarGridSpec(
            num_scalar_prefetch=2,
            grid=(bs,),
            in_specs=[hspec(0), hspec(1), hspec(2),
                      pl.BlockSpec(memory_space=pl.ANY), pl.BlockSpec(memory_space=pl.ANY),
                      whole(bias_near), whole(bias_far)],
            out_specs=pl.BlockSpec((1, n_heads, 1, hd), lambda b, pid, nr: (b, 0, 0, 0)),
            scratch_shapes=[buf, buf, pltpu.SemaphoreType.DMA((2,))],
        ),
        out_shape=jax.ShapeDtypeStruct((bs, n_heads, 1, hd), BF16),
        compiler_params=_cparams("arbitrary"),
        name="sample_attn",
    )(sel_pages, near, hs3, hs3, hs3, cache_k, cache_v, bias_near, bias_far)


def _ret_sample_kernel(q_ref, k_ref, v_ref, g_ref, cos_ref, sin_ref, dec_ref, s0_ref, o_ref, s_ref):
    n_heads = s_ref.shape[1]
    scale = HEAD_DIM ** -0.5
    cos = cos_ref[...]
    sin = sin_ref[...]
    rows = lax.broadcasted_iota(jnp.int32, (HEAD_DIM, HEAD_DIM), 0)
    cols = lax.broadcasted_iota(jnp.int32, (HEAD_DIM, HEAD_DIM), 1)
    eye = jnp.where(rows == cols, 1.0, 0.0)
    for hh in range(n_heads):
        sl = slice(hh * HEAD_DIM, (hh + 1) * HEAD_DIM)
        q = _rotate(q_ref[0, :, sl], cos, sin)
        k = _rotate(k_ref[0, :, sl], cos, sin) * scale
        v = v_ref[0, :, sl]
        g = g_ref[0, :, sl]
        dec = dec_ref[hh:hh + 1, :]
        q_col = jnp.sum(eye * q, axis=-1, keepdims=True)
        k_col = jnp.sum(eye * k, axis=-1, keepdims=True)
        s = s0_ref[0, hh]
        qk = jnp.sum(q * k, axis=-1, keepdims=True)
        o = qk * v + jnp.sum(q_col * s, axis=0, keepdims=True) * dec
        s_ref[0, hh] = s * dec + k_col * v
        o_ref[0, :, sl] = _head_norm_gate(o, g).astype(o_ref.dtype)


def _ret_sample(hs, cos, sin, decay, s0, *, n_heads, col0):
    bs = hs.shape[0]
    width = n_heads * HEAD_DIM
    hspec = lambda g: pl.BlockSpec((1, 1, width), lambda b: (b, 0, col0 + g))
    vspec = pl.BlockSpec((1, HEAD_DIM), lambda b: (0, 0))
    sspec = pl.BlockSpec((1, n_heads, HEAD_DIM, HEAD_DIM), lambda b: (b, 0, 0, 0))
    return pl.pallas_call(
        _ret_sample_kernel,
        grid=(bs,),
        in_specs=[hspec(0), hspec(1), hspec(2), hspec(3), vspec, vspec,
                  pl.BlockSpec((n_heads, HEAD_DIM), lambda b: (0, 0)), sspec],
        out_specs=[pl.BlockSpec((1, 1, width), lambda b: (b, 0, 0)), sspec],
        out_shape=[jax.ShapeDtypeStruct((bs, 1, width), BF16),
                   jax.ShapeDtypeStruct((bs, n_heads, HEAD_DIM, HEAD_DIM), F32)],
        compiler_params=_cparams("parallel"),
        name="ret_sample",
    )(hs, hs, hs, hs, cos, sin, decay, s0)


def _t5_bucket(rel):
    n = jnp.maximum(rel, 0)
    max_exact = NUM_BUCKETS // 2
    nf = jnp.maximum(n, max_exact).astype(F32)
    large = max_exact + (jnp.log(nf / max_exact) / math.log(MAX_DISTANCE / max_exact)
                         * (NUM_BUCKETS - max_exact)).astype(jnp.int32)
    return jnp.where(n < max_exact, n, jnp.minimum(large, NUM_BUCKETS - 1))


def _rope_tables(pos):
    half = HEAD_DIM // 2
    inv = ROPE_BASE ** (-jnp.arange(half, dtype=F32) / half)
    ang = pos[:, None] * inv[None, :]
    cos = jnp.cos(ang)
    sin = jnp.sin(ang)
    return jnp.concatenate([cos, cos], -1), jnp.concatenate([-sin, sin], -1)


def _decay_tables(n_heads, chunk):
    lg = jnp.log(1.0 - 2.0 ** (-5.0 - jnp.arange(n_heads, dtype=F32)))
    i = jnp.arange(chunk, dtype=F32)
    diff = i[:, None] - i[None, :]
    dmask = jnp.where(diff >= 0, jnp.exp(lg[:, None, None] * jnp.maximum(diff, 0.0)), 0.0)
    full = lambda col: jnp.broadcast_to(col[..., None], (n_heads, chunk, HEAD_DIM))
    cross = full(jnp.exp(lg[:, None] * (i + 1.0)))
    upd = full(jnp.exp(lg[:, None] * (chunk - 1.0 - i)))
    cdec = jnp.broadcast_to(jnp.exp(lg * chunk)[:, None, None], (n_heads, chunk, HEAD_DIM))
    return dmask, cross, upd, cdec


def kernel(x_prompt, x_sample, cache_k, cache_v, state_ret, state_conv, page_table, p_prompt, p_sample,
           rel_bias, w_in, w_out, ln1_g, ln1_b, w_gate, w_up, conv_w, conv_b, w_down, ln2_g, ln2_b,
           w_ple, w_ple_gate):
    bp, seq, d = x_prompt.shape
    bs, dec_seq, _ = x_sample.shape
    depth, n_pool, page, n_heads, hd = cache_k.shape
    n_pages = page_table.shape[1]
    past_len = n_pages * page
    nf = w_gate.shape[2]
    attn_w = n_heads * hd
    assert hd == HEAD_DIM and dec_seq == 1 and MOBA_BLOCK % page == 0
    assert seq % MOBA_BLOCK == 0 and past_len % MOBA_BLOCK == 0 and past_len // MOBA_BLOCK >= MOBA_TOPK
    assert w_in.shape[2] == 7 * attn_w and d == 2 * attn_w and RET_CHUNK == HEAD_DIM
    assert int(np.floor(np.log(np.float32(MOBA_BLOCK + 1) / 16) / math.log(MAX_DISTANCE / 16) * 16)) >= 15
    alpha = (2 * depth) ** 0.25
    ppb = MOBA_BLOCK // page
    n_past_blocks = past_len // MOBA_BLOCK
    mp = bp * seq

    w_down_b = w_down.astype(BF16)
    vec3 = lambda a: a.reshape(depth, 1, a.shape[-1])
    ln1_g3, ln1_b3, ln2_g3, ln2_b3, conv_b3 = map(vec3, (ln1_g, ln1_b, ln2_g, ln2_b, conv_b))

    def bias_of(rel):
        onehot = (_t5_bucket(rel)[..., None] == jnp.arange(NUM_BUCKETS)).astype(F32)
        return jnp.einsum("...b,bh->h...", onehot, rel_bias.astype(F32), precision=lax.Precision.HIGHEST)

    ii = jnp.arange(MOBA_BLOCK, dtype=jnp.int32)
    far = rel_bias[NUM_BUCKETS - 1].astype(F32)
    bias_diag = bias_of(ii[:, None] - ii[None, :]) * LOG2E
    bias_adj = bias_of(MOBA_BLOCK + ii[:, None] - ii[None, :]) * LOG2E
    bias_far = jnp.broadcast_to((far * LOG2E)[:, None, None], (n_heads, 1, 128))
    bias_near_s = bias_of(MOBA_BLOCK - ii).reshape(n_heads, ppb, page)
    bias_far_s = jnp.stack([jnp.broadcast_to(far[:, None], (n_heads, 128)),
                            jnp.broadcast_to(rel_bias[0].astype(F32)[:, None], (n_heads, 128))], axis=1)

    cos_p, sin_p = _rope_tables(jnp.arange(seq, dtype=F32))
    cos_s, sin_s = _rope_tables(jnp.full((1,), float(past_len), F32))
    tabs = _decay_tables(n_heads, RET_CHUNK)
    decay_s = jnp.broadcast_to(
        (1.0 - 2.0 ** (-5.0 - jnp.arange(n_heads, dtype=F32)))[:, None], (n_heads, HEAD_DIM))
    decay_s = jnp.exp(jnp.log(decay_s))

    ck = cache_k.reshape(depth, n_pool, page * n_heads, hd)
    page_table_flat = page_table.reshape(-1)

    xp_f = x_prompt.reshape(mp, d)
    xp_b = xp_f.astype(BF16)
    xs_f = x_sample.reshape(bs, d)
    xs_b = xs_f.astype(BF16)
    s0_prompt = jnp.zeros((bp, n_heads, hd, hd), F32)
    conv0_prompt = jnp.zeros((bp, CONV_W - 1, nf), F32)

    pp3 = p_prompt.reshape(depth, mp, -1)
    ps3 = p_sample.reshape(depth, bs, -1)
    outs = {k: [] for k in ("ks", "vs", "rp", "rs", "cp", "cs")}
    kp_all = vp_all = None
    for l in range(depth):
        xs_pad = jnp.pad(xs_b, ((0, SAMPLE_ROWS_PAD - bs), (0, 0)))
        proj = functools.partial(_matmul, xp_b, xs_pad, w_in, l, tn=1024)
        tm_kv = 1024 if l else 512
        qp, qs = proj(col0=0, n=attn_w, tm=1024)
        kp_all, ks = proj(col0=attn_w, n=attn_w, tm=tm_kv, stack=kp_all, stacked=True)
        vp_all, vs = proj(col0=2 * attn_w, n=attn_w, tm=tm_kv, stack=vp_all, stacked=True)
        hr, rs = proj(col0=3 * attn_w, n=4 * attn_w, tm=1024)
        attn = _moba_prompt(qp, kp_all, vp_all, l, bias_diag, bias_adj, bias_far,
                            batch=bp, seq=seq, n_heads=n_heads)
        ret, s_new = _ret_prompt(hr, cos_p, sin_p, tabs, s0_prompt, batch=bp, seq=seq, n_heads=n_heads, col0=0)
        outs["rp"].append(s_new)
        x1_f, x1_b = _outproj_ln(attn, ret, w_out, l, xp_f, ln1_g3, ln1_b3, alpha=alpha, tm=512)
        gact, conv_new, ksum_l = _ffn_up_prompt(x1_b, w_gate, w_up, conv_w, conv_b3, conv0_prompt, l,
                                                page_table_flat, ck, batch=bp, seq=seq, tm=1024, tn=512,
                                                pages_per_block=ppb, n_heads=n_heads)
        ksum_l = ksum_l.reshape(bs, n_past_blocks, n_heads, hd)
        outs["cp"].append(conv_new)
        x2 = _ffn_down_ln(gact, w_down_b, l, x1_f, ln2_g3, ln2_b3, alpha=alpha, tm=512, tk=nf // 2)
        xp_f, xp_b = _ple(x2, pp3, w_ple_gate, w_ple, l, tm=512)

        hs = jnp.concatenate([qs, ks, vs, rs], axis=1)[:bs]
        outs["ks"].append(hs[:, attn_w:2 * attn_w].reshape(bs, 1, n_heads, hd))
        outs["vs"].append(hs[:, 2 * attn_w:3 * attn_w].reshape(bs, 1, n_heads, hd))
        hs3 = hs.reshape(bs, 1, -1)
        picks = _sample_select(hs[:, :attn_w].reshape(bs, n_heads, hd), ksum_l)
        blocks = jnp.transpose(picks[..., 0], (0, 2, 1))
        sel_pos = blocks[..., None] * ppb + jnp.arange(ppb, dtype=jnp.int32)
        sel_pages = page_table[jnp.arange(bs)[:, None, None, None], sel_pos]
        near = (blocks == n_past_blocks - 1).astype(jnp.int32)
        attn_s = _sample_attn(hs3, cache_k, cache_v, l, sel_pages.reshape(-1), near.reshape(-1),
                              bias_near_s, bias_far_s, n_sel_pages=MOBA_TOPK * ppb)
        ret_s, s_new_s = _ret_sample(hs3, cos_s, sin_s, decay_s, state_ret[l], n_heads=n_heads, col0=3)
        outs["rs"].append(s_new_s)
        x1s_f, x1s_b = _outproj_ln(attn_s.reshape(bs, attn_w), ret_s.reshape(bs, attn_w), w_out, l, xs_f,
                                   ln1_g3, ln1_b3, alpha=alpha, tm=bs)
        gact_s, c0, c1 = _ffn_up_sample(x1s_b, w_gate, w_up, conv_w, conv_b3,
                                        state_conv[l, :, 0], state_conv[l, :, 1], l, tn=512)
        outs["cs"].append(jnp.stack([c0, c1], axis=1))
        x2s = _ffn_down_ln(gact_s, w_down_b, l, x1s_f, ln2_g3, ln2_b3, alpha=alpha, tm=bs, tk=nf // 4)
        xs_f, xs_b = _ple(x2s, ps3, w_ple_gate, w_ple, l, tm=bs)

    stack = lambda key: jnp.stack(outs[key])
    return (xp_f.reshape(bp, seq, d), xs_f.reshape(bs, 1, d),
            kp_all.reshape(depth, bp, seq, n_heads, hd), vp_all.reshape(depth, bp, seq, n_heads, hd),
            stack("ks"), stack("vs"), stack("rp"), stack("rs"), stack("cp"), stack("cs"))
```

```python
import functools
import math

import jax
import jax.numpy as jnp
import numpy as np
from jax import lax
from jax.experimental import pallas as pl
from jax.experimental.pallas import tpu as pltpu

HEAD_DIM = 128
MOBA_BLOCK = 256
MOBA_TOPK = 3
RET_CHUNK = 128
ROPE_BASE = 10000.0
NUM_BUCKETS = 32
MAX_DISTANCE = 128
CONV_W = 3
SAMPLE_ROWS_PAD = 16
LN_EPS = 1e-5

VMEM_LIMIT_BYTES = 56 * 1024 * 1024
NEG_BIG = -1e30
LOG2E = math.log2(math.e)

BF16 = jnp.bfloat16
F32 = jnp.float32

_NT = (((1,), (1,)), ((), ()))
_TN = (((0,), (0,)), ((), ()))


def _cparams(*sem):
    return pltpu.CompilerParams(dimension_semantics=sem, vmem_limit_bytes=VMEM_LIMIT_BYTES)


def _sigmoid(x):
    return 1.0 / (1.0 + jnp.exp(-x))


def _layer_norm_rows(y, g, b):
    mu = jnp.mean(y, axis=-1, keepdims=True)
    d = y - mu
    var = jnp.mean(d * d, axis=-1, keepdims=True)
    return d * lax.rsqrt(var + LN_EPS) * g + b


def _split_bf16(a):
    hi = a.astype(BF16)
    lo = (a - hi.astype(F32)).astype(BF16)
    return hi, lo


def _dot_nt_precise(a, b):
    ah, al = _split_bf16(a)
    bh, bl = _split_bf16(b)
    dot = functools.partial(lax.dot_general, dimension_numbers=_NT, preferred_element_type=F32)
    return dot(ah, bh) + (dot(ah, bl) + dot(al, bh))


def _mm_kernel(slab, x_ref, w_ref, xs_ref, *rest):
    o_ref, os_ref, wb_ref = rest[-3:]
    i = pl.program_id(1)

    @pl.when(i == 0)
    def _():
        wb_ref[...] = w_ref[0].astype(BF16)

    res = jnp.dot(x_ref[...], wb_ref[...], preferred_element_type=F32)
    if slab is None:
        o_ref[...] = res
    else:
        for other in range(o_ref.shape[0]):
            if other != slab:
                o_ref[other] = jnp.zeros(o_ref.shape[1:], o_ref.dtype)
        o_ref[slab] = res

    @pl.when(i == pl.num_programs(1) - 1)
    def _():
        os_ref[...] = jnp.dot(xs_ref[...], wb_ref[...], preferred_element_type=F32)


def _matmul(x, xs, w, layer, *, col0, n, tm, tn, stack=None, stacked=False):
    m, k = x.shape
    ms = xs.shape[0]
    depth = w.shape[0]
    assert col0 % tn == 0 and n % tn == 0
    in_specs = [pl.BlockSpec((tm, k), lambda j, i: (i, 0)),
                pl.BlockSpec((1, k, tn), lambda j, i: (layer, 0, col0 // tn + j)),
                pl.BlockSpec((ms, k), lambda j, i: (0, 0))]
    operands, aliases, slab = [x, w, xs], {}, None
    if not stacked:
        out_spec = pl.BlockSpec((tm, tn), lambda j, i: (i, j))
        out_shape = jax.ShapeDtypeStruct((m, n), F32)
    else:
        out_shape = jax.ShapeDtypeStruct((depth, m, n), F32)
        if stack is None:
            slab = layer
            out_spec = pl.BlockSpec((depth, tm, tn), lambda j, i: (0, i, j))
        else:
            in_specs.append(pl.BlockSpec(memory_space=pl.ANY))
            operands, aliases, slab = [x, w, xs, stack], {3: 0}, 0
            out_spec = pl.BlockSpec((1, tm, tn), lambda j, i: (layer, i, j))
    return pl.pallas_call(
        functools.partial(_mm_kernel, slab),
        grid=(n // tn, m // tm),
        in_specs=in_specs,
        out_specs=[out_spec, pl.BlockSpec((ms, tn), lambda j, i: (0, j))],
        out_shape=[out_shape, jax.ShapeDtypeStruct((ms, n), F32)],
        scratch_shapes=[pltpu.VMEM((k, tn), BF16)],
        input_output_aliases=aliases,
        compiler_params=_cparams("parallel", "arbitrary"),
        name="proj_in",
    )(*operands)


def _moba_select(cc, qf, km_ref):
    tq = qf.shape[0]
    nbp = km_ref.shape[0]
    gate = _dot_nt_precise(km_ref[...], qf)
    blk = lax.broadcasted_iota(jnp.int32, (nbp, tq), 0)
    eligible = blk < cc
    gate = jnp.where(eligible, gate, -jnp.inf)
    rank = jnp.zeros((nbp, tq), F32)
    for mth in range(cc):
        gm = gate[mth:mth + 1, :]
        beats = jnp.logical_or(gm > gate, jnp.logical_and(gm == gate, mth < blk))
        rank = rank + jnp.where(beats, 1.0, 0.0)
    sel_t = jnp.where(jnp.logical_and(eligible, rank < MOBA_TOPK), 1.0, 0.0)
    sel_pad = jnp.concatenate([sel_t, jnp.zeros((128 - nbp, tq), F32)], axis=0).astype(BF16)
    rows = lax.broadcasted_iota(jnp.int32, (tq, tq), 0)
    cols = lax.broadcasted_iota(jnp.int32, (tq, tq), 1)
    eye = jnp.where(rows == cols, 1.0, 0.0).astype(BF16)
    return lax.dot_general(eye, sel_pad, _NT, preferred_element_type=F32)


def _moba_tile(cc, q_ref, bd_ref, ba_ref, bf_ref, o_ref, kb_ref, vb_ref, km_ref):
    tq = q_ref.shape[0]
    scale2 = HEAD_DIM ** -0.5 * LOG2E
    nk = (cc + 1) * MOBA_BLOCK
    rows = lax.broadcasted_iota(jnp.int32, (tq, MOBA_BLOCK), 0)
    cols = lax.broadcasted_iota(jnp.int32, (tq, MOBA_BLOCK), 1)
    for hp in range(q_ref.shape[1] // HEAD_DIM):
        sl = slice(hp * HEAD_DIM, (hp + 1) * HEAD_DIM)
        qf = q_ref[:, sl]
        s = lax.dot_general(qf.astype(BF16), kb_ref[0:nk, sl], _NT, preferred_element_type=F32) * scale2
        sel = _moba_select(cc, qf, km_ref.at[hp]) if cc > MOBA_TOPK else None
        far_bias = bf_ref[hp, 0:1, 0:1]
        pieces = []
        for n in range(cc + 1):
            sn = s[:, n * MOBA_BLOCK:(n + 1) * MOBA_BLOCK]
            if n == cc:
                sn = jnp.where(cols <= rows, sn + bd_ref[hp], NEG_BIG)
            elif n == cc - 1:
                sn = sn + ba_ref[hp]
            else:
                sn = sn + far_bias
            if n < cc and sel is not None:
                sn = jnp.where(sel[:, n:n + 1] > 0.5, sn, NEG_BIG)
            pieces.append(sn)
        s = jnp.concatenate(pieces, axis=1) if cc else pieces[0]
        m = jnp.max(s, axis=-1, keepdims=True)
        p = jnp.exp2(s - m)
        l = jnp.sum(p, axis=-1, keepdims=True)
        acc = jnp.dot(p.astype(BF16), vb_ref[0:nk, sl], preferred_element_type=F32)
        o_ref[:, sl] = (acc / l).astype(o_ref.dtype)


def _moba_prompt_kernel(q_ref, k_ref, v_ref, bd_ref, ba_ref, bf_ref, o_ref, kb_ref, vb_ref, km_ref):
    c = pl.program_id(2)
    nb = k_ref.shape[1] // MOBA_BLOCK

    @pl.when(c == 0)
    def _():
        kb_ref[...] = k_ref[0].astype(BF16)
        vb_ref[...] = v_ref[0].astype(BF16)
        km_ref[...] = jnp.zeros_like(km_ref)
        for hp in range(km_ref.shape[0]):
            sl = slice(hp * HEAD_DIM, (hp + 1) * HEAD_DIM)
            for n in range(nb):
                km_ref[hp, n:n + 1, :] = jnp.mean(k_ref[0, n * MOBA_BLOCK:(n + 1) * MOBA_BLOCK, sl],
                                                  axis=0, keepdims=True)

    for cc in range(nb):
        pl.when(c == cc)(functools.partial(_moba_tile, cc, q_ref, bd_ref, ba_ref, bf_ref, o_ref,
                                           kb_ref, vb_ref, km_ref))


MOBA_HEADS_PER_STEP = 2


def _moba_prompt(q, k, v, layer, bias_diag, bias_adj, bias_far, *, batch, seq, n_heads):
    nq = seq // MOBA_BLOCK
    nbias = bias_far.shape[1]
    hps = MOBA_HEADS_PER_STEP
    width = hps * HEAD_DIM
    kvspec = pl.BlockSpec((1, seq, width), lambda b, hh, qi: (layer, b, hh))
    bspec = pl.BlockSpec((hps, MOBA_BLOCK, MOBA_BLOCK), lambda b, hh, qi: (hh, 0, 0))
    return pl.pallas_call(
        _moba_prompt_kernel,
        grid=(batch, n_heads // hps, nq),
        in_specs=[
            pl.BlockSpec((MOBA_BLOCK, width), lambda b, hh, qi: (b * nq + qi, hh)),
            kvspec, kvspec, bspec, bspec,
            pl.BlockSpec((hps, nbias, 128), lambda b, hh, qi: (hh, 0, 0)),
        ],
        out_specs=pl.BlockSpec((MOBA_BLOCK, width), lambda b, hh, qi: (b * nq + qi, hh)),
        out_shape=jax.ShapeDtypeStruct((batch * seq, n_heads * HEAD_DIM), BF16),
        scratch_shapes=[
            pltpu.VMEM((seq, width), BF16),
            pltpu.VMEM((seq, width), BF16),
            pltpu.VMEM((hps, max(16, seq // MOBA_BLOCK), HEAD_DIM), F32),
        ],
        compiler_params=_cparams("parallel", "parallel", "arbitrary"),
        name="moba_prompt",
    )(q, k, v, bias_diag, bias_adj, bias_far)


def _rotate(x, cos, sin_signed):
    return x * cos + pltpu.roll(x, HEAD_DIM // 2, 1) * sin_signed


def _head_norm_gate(o, g):
    mu = jnp.mean(o, axis=-1, keepdims=True)
    d = o - mu
    var = jnp.mean(d * d, axis=-1, keepdims=True)
    return g * _sigmoid(g) * (d * lax.rsqrt(var + LN_EPS))


def _ret_prompt_kernel(q_ref, k_ref, v_ref, g_ref, cos_ref, sin_ref, dm_ref, cr_ref, up_ref, cd_ref,
                       s0_ref, o_ref, s_ref):
    n_heads = s_ref.shape[1]
    scale = HEAD_DIM ** -0.5

    @pl.when(pl.program_id(1) == 0)
    def _():
        s_ref[...] = s0_ref[...]

    cos = cos_ref[...]
    sin = sin_ref[...]
    for hh in range(n_heads):
        sl = slice(hh * HEAD_DIM, (hh + 1) * HEAD_DIM)
        q = _rotate(q_ref[:, sl], cos, sin)
        k = _rotate(k_ref[:, sl], cos, sin) * scale
        qb = q.astype(BF16)
        kb = k.astype(BF16)
        vb = v_ref[:, sl].astype(BF16)
        s = s_ref[0, hh]
        sc = lax.dot_general(qb, kb, _NT, preferred_element_type=F32) * dm_ref[hh]
        o = jnp.dot(sc.astype(BF16), vb, preferred_element_type=F32)
        o = o + jnp.dot(qb, s.astype(BF16), preferred_element_type=F32) * cr_ref[hh]
        ku = (k * up_ref[hh]).astype(BF16)
        s_ref[0, hh] = s * cd_ref[hh] + lax.dot_general(ku, vb, _TN, preferred_element_type=F32)
        o_ref[:, sl] = _head_norm_gate(o, g_ref[:, sl]).astype(o_ref.dtype)


def _ret_prompt(h, cos, sin, tabs, s0, *, batch, seq, n_heads, col0):
    nc = seq // RET_CHUNK
    width = n_heads * HEAD_DIM
    hspec = lambda g: pl.BlockSpec((RET_CHUNK, width), lambda b, c: (b * nc + c, col0 + g))
    tspec = pl.BlockSpec((n_heads, RET_CHUNK, HEAD_DIM), lambda b, c: (0, 0, 0))
    sspec = pl.BlockSpec((1, n_heads, HEAD_DIM, HEAD_DIM), lambda b, c: (b, 0, 0, 0))
    return pl.pallas_call(
        _ret_prompt_kernel,
        grid=(batch, nc),
        in_specs=[hspec(0), hspec(1), hspec(2), hspec(3),
                  pl.BlockSpec((RET_CHUNK, HEAD_DIM), lambda b, c: (c, 0)),
                  pl.BlockSpec((RET_CHUNK, HEAD_DIM), lambda b, c: (c, 0)),
                  tspec, tspec, tspec, tspec, sspec],
        out_specs=[pl.BlockSpec((RET_CHUNK, width), lambda b, c: (b * nc + c, 0)), sspec],
        out_shape=[jax.ShapeDtypeStruct((batch * seq, width), BF16),
                   jax.ShapeDtypeStruct((batch, n_heads, HEAD_DIM, HEAD_DIM), F32)],
        compiler_params=_cparams("parallel", "arbitrary"),
        name="ret_prompt",
    )(h, h, h, h, cos, sin, *tabs, s0)


def _outproj_kernel(alpha, a_ref, r_ref, wa_ref, wr_ref, x_ref, g_ref, b_ref, of_ref, ob_ref, wb_ref):
    @pl.when(pl.program_id(0) == 0)
    def _():
        wb_ref[0] = wa_ref[0].astype(BF16)
        wb_ref[1] = wr_ref[0].astype(BF16)

    mix = jnp.dot(a_ref[...], wb_ref[0], preferred_element_type=F32)
    mix = mix + jnp.dot(r_ref[...], wb_ref[1], preferred_element_type=F32)
    y = _layer_norm_rows(alpha * x_ref[...] + mix, g_ref[0], b_ref[0])
    of_ref[...] = y
    ob_ref[...] = y.astype(BF16)


def _outproj_ln(attn, ret, w_out, layer, x, g, b, *, alpha, tm):
    m, ka = attn.shape
    d = x.shape[1]
    row = lambda width: pl.BlockSpec((tm, width), lambda i: (i, 0))
    vec = pl.BlockSpec((1, 1, d), lambda i: (layer, 0, 0))
    return pl.pallas_call(
        functools.partial(_outproj_kernel, alpha),
        grid=(m // tm,),
        in_specs=[row(ka), row(ka),
                  pl.BlockSpec((1, ka, d), lambda i: (layer, 0, 0), pipeline_mode=pl.Buffered(1)),
                  pl.BlockSpec((1, ka, d), lambda i: (layer, 1, 0), pipeline_mode=pl.Buffered(1)),
                  row(d), vec, vec],
        out_specs=[row(d), row(d)],
        out_shape=[jax.ShapeDtypeStruct((m, d), F32), jax.ShapeDtypeStruct((m, d), BF16)],
        scratch_shapes=[pltpu.VMEM((2, ka, d), BF16)],
        compiler_params=_cparams("arbitrary"),
        name="outproj_ln",
    )(attn, ret, w_out, w_out, x, g, b)


KSUM_PAGES_PER_STEP = 16


def _ffn_up_prompt_kernel(tiles_per_seq, n_pages, ids_ref, x_ref, wg_ref, wu_ref, cw_ref, cb_ref, buf_ref,
                          *rest):
    del ids_ref
    page_refs = rest[:n_pages]
    o_ref, cn_ref, ks_ref, carry_ref, wgb_ref, wub_ref = rest[n_pages:]
    i = pl.program_id(1)
    tm = x_ref.shape[0]

    @pl.when(i == 0)
    def _():
        wgb_ref[...] = wg_ref[0].astype(BF16)
        wub_ref[...] = wu_ref[0].astype(BF16)

    x = x_ref[...]
    a = jnp.dot(x, wgb_ref[...], preferred_element_type=F32)
    u = jnp.dot(x, wub_ref[...], preferred_element_type=F32)
    first = (i % tiles_per_seq) == 0
    prev = jnp.where(first, buf_ref[0], carry_ref[6:8, :])
    row = lax.broadcasted_iota(jnp.int32, a.shape, 0)
    a1 = jnp.where(row == 0, prev[1:2, :], pltpu.roll(a, 1, 0))
    a2 = jnp.where(row == 0, prev[0:1, :], jnp.where(row == 1, prev[1:2, :], pltpu.roll(a, 2, 0)))
    cw = cw_ref[0]
    conv = cb_ref[0] + cw[0:1, :] * a2 + cw[1:2, :] * a1 + cw[2:3, :] * a
    o_ref[...] = (conv * _sigmoid(conv) * u).astype(o_ref.dtype)
    carry_ref[...] = a[tm - 8:tm, :]

    @pl.when((i % tiles_per_seq) == tiles_per_seq - 1)
    def _():
        cn_ref[0] = a[tm - (CONV_W - 1):tm, :]

    n_heads = ks_ref.shape[1]
    ppb = n_pages // ks_ref.shape[0]
    for r, p_ref in enumerate(page_refs):
        pg = p_ref[0, 0]
        part = jnp.sum(pg.reshape(pg.shape[0] // n_heads, n_heads, pg.shape[1]), axis=0)
        if r % ppb == 0:
            ks_ref[r // ppb] = part
        else:
            ks_ref[r // ppb] += part


def _ffn_up_prompt(x, w_gate, w_up, conv_w, conv_b, conv_buf, layer, page_table_flat, cache_k_rows, *,
                   batch, seq, tm, tn, pages_per_block, n_heads):
    m, d = x.shape
    nf = w_gate.shape[2]
    tps = seq // tm
    n_i = m // tm
    _, _, rows, hd = cache_k_rows.shape
    pps = KSUM_PAGES_PER_STEP
    n_pages = page_table_flat.shape[0]
    n_groups = n_pages // pps
    bps = pps // pages_per_block
    assert n_pages % pps == 0 and pps % pages_per_block == 0 and n_groups <= (nf // tn) * n_i
    group = lambda j, i: jnp.minimum(j * n_i + i, n_groups - 1)
    page_spec = lambda r: pl.BlockSpec(
        (1, 1, rows, hd), lambda j, i, ids: (layer, ids[pps * group(j, i) + r], 0, 0))
    wspec = pl.BlockSpec((1, d, tn), lambda j, i, ids: (layer, 0, j))
    return pl.pallas_call(
        functools.partial(_ffn_up_prompt_kernel, tps, pps),
        grid_spec=pltpu.PrefetchScalarGridSpec(
            num_scalar_prefetch=1,
            grid=(nf // tn, n_i),
            in_specs=[pl.BlockSpec((tm, d), lambda j, i, ids: (i, 0)), wspec, wspec,
                      pl.BlockSpec((1, CONV_W, tn), lambda j, i, ids: (layer, 0, j)),
                      pl.BlockSpec((1, 1, tn), lambda j, i, ids: (layer, 0, j)),
                      pl.BlockSpec((1, CONV_W - 1, tn), lambda j, i, ids: (i // tps, 0, j))]
            + [page_spec(r) for r in range(pps)],
            out_specs=[pl.BlockSpec((tm, tn), lambda j, i, ids: (i, j)),
                       pl.BlockSpec((1, CONV_W - 1, tn), lambda j, i, ids: (i // tps, 0, j)),
                       pl.BlockSpec((bps, n_heads, hd), lambda j, i, ids: (group(j, i), 0, 0))],
            scratch_shapes=[pltpu.VMEM((8, tn), F32), pltpu.VMEM((d, tn), BF16), pltpu.VMEM((d, tn), BF16)],
        ),
        out_shape=[jax.ShapeDtypeStruct((m, nf), BF16),
                   jax.ShapeDtypeStruct((batch, CONV_W - 1, nf), F32),
                   jax.ShapeDtypeStruct((n_pages // pages_per_block, n_heads, hd), F32)],
        compiler_params=_cparams("arbitrary", "arbitrary"),
        name="ffn_up_prompt",
    )(page_table_flat, x, w_gate, w_up, conv_w, conv_b, conv_buf, *([cache_k_rows] * pps))


def _ffn_up_sample_kernel(x_ref, wg_ref, wu_ref, cw_ref, cb_ref, b0_ref, b1_ref, o_ref, n0_ref, n1_ref):
    x = x_ref[...]
    a = jnp.dot(x, wg_ref[0].astype(BF16), preferred_element_type=F32)
    u = jnp.dot(x, wu_ref[0].astype(BF16), preferred_element_type=F32)
    cw = cw_ref[0]
    conv = cb_ref[0] + cw[0:1, :] * b0_ref[...] + cw[1:2, :] * b1_ref[...] + cw[2:3, :] * a
    o_ref[...] = (conv * _sigmoid(conv) * u).astype(o_ref.dtype)
    n0_ref[...] = b1_ref[...]
    n1_ref[...] = a


def _ffn_up_sample(x, w_gate, w_up, conv_w, conv_b, buf0, buf1, layer, *, tn):
    m, d = x.shape
    nf = w_gate.shape[2]
    wspec = pl.BlockSpec((1, d, tn), lambda j: (layer, 0, j))
    cspec = pl.BlockSpec((m, tn), lambda j: (0, j))
    return pl.pallas_call(
        _ffn_up_sample_kernel,
        grid=(nf // tn,),
        in_specs=[pl.BlockSpec((m, d), lambda j: (0, 0)), wspec, wspec,
                  pl.BlockSpec((1, CONV_W, tn), lambda j: (layer, 0, j)),
                  pl.BlockSpec((1, 1, tn), lambda j: (layer, 0, j)),
                  cspec, cspec],
        out_specs=[cspec, cspec, cspec],
        out_shape=[jax.ShapeDtypeStruct((m, nf), BF16),
                   jax.ShapeDtypeStruct((m, nf), F32),
                   jax.ShapeDtypeStruct((m, nf), F32)],
        compiler_params=_cparams("parallel"),
        name="ffn_up_sample",
    )(x, w_gate, w_up, conv_w, conv_b, buf0, buf1)


def _ln_ple_kernel(alpha, f_ref, x_ref, lg_ref, lb_ref, p_ref, wg_ref, wp_ref, of_ref, ob_ref):
    x2 = _layer_norm_rows(alpha * x_ref[...] + f_ref[...], lg_ref[0], lb_ref[0])
    gate = _sigmoid(jnp.dot(x2.astype(BF16), wg_ref[0], preferred_element_type=F32))
    emb = jnp.dot(p_ref[0].astype(BF16), wp_ref[0].astype(BF16), preferred_element_type=F32)
    y = x2 + gate * emb
    of_ref[...] = y
    ob_ref[...] = y.astype(BF16)


def _ln_ple(f, x, ln_g, ln_b, p, w_ple_gate_b, w_ple, layer, *, alpha, tm):
    m, d = x.shape
    pd = p.shape[2]
    row = lambda width: pl.BlockSpec((tm, width), lambda i: (i, 0))
    vec = pl.BlockSpec((1, 1, d), lambda i: (layer, 0, 0))
    return pl.pallas_call(
        functools.partial(_ln_ple_kernel, alpha),
        grid=(m // tm,),
        in_specs=[row(d), row(d), vec, vec, pl.BlockSpec((1, tm, pd), lambda i: (layer, i, 0)),
                  pl.BlockSpec((1, d, d), lambda i: (layer, 0, 0), pipeline_mode=pl.Buffered(1)),
                  pl.BlockSpec((1, pd, d), lambda i: (layer, 0, 0))],
        out_specs=[row(d), row(d)],
        out_shape=[jax.ShapeDtypeStruct((m, d), F32), jax.ShapeDtypeStruct((m, d), BF16)],
        compiler_params=_cparams("parallel"),
        name="ln_ple",
    )(f, x, ln_g, ln_b, p, w_ple_gate_b, w_ple)


def _sample_select_kernel(q_ref, ks_ref, o_ref):
    nblk, n_heads = ks_ref.shape[1], ks_ref.shape[2]
    prod = ks_ref[0] * q_ref[0][None] * (1.0 / MOBA_BLOCK)
    gate = jnp.sum(prod, axis=-1, keepdims=True)
    blk = lax.broadcasted_iota(jnp.int32, gate.shape, 0).astype(F32)
    for t in range(MOBA_TOPK):
        best = jnp.max(gate, axis=0, keepdims=True)
        idx = jnp.min(jnp.where(gate == best, blk, float(nblk)), axis=0, keepdims=True)
        o_ref[0, t] = jnp.broadcast_to(idx[0], (n_heads, 128)).astype(jnp.int32)
        gate = jnp.where(blk == idx, -jnp.inf, gate)


def _sample_select(q_heads, ksum_l):
    bs, nblk, n_heads, hd = ksum_l.shape
    return pl.pallas_call(
        _sample_select_kernel,
        grid=(bs,),
        in_specs=[pl.BlockSpec((1, n_heads, hd), lambda b: (b, 0, 0)),
                  pl.BlockSpec((1, nblk, n_heads, hd), lambda b: (b, 0, 0, 0))],
        out_specs=pl.BlockSpec((1, MOBA_TOPK, n_heads, 128), lambda b: (b, 0, 0, 0)),
        out_shape=jax.ShapeDtypeStruct((bs, MOBA_TOPK, n_heads, 128), jnp.int32),
        compiler_params=_cparams("parallel"),
        name="sample_select",
    )(q_heads, ksum_l)


def _sample_attn_kernel(layer, n_sel_pages, pid_ref, near_ref, q_ref, kn_ref, vn_ref, ck_hbm, cv_hbm,
                        bn_ref, bf_ref, o_ref, kbuf, vbuf, sem):
    b = pl.program_id(0)
    n_seq = pl.num_programs(0)
    n_heads = o_ref.shape[1]
    ppb, page = bn_ref.shape[1], bn_ref.shape[2]
    scale = HEAD_DIM ** -0.5

    def page_copies(seq, slot):
        copies = []
        for hh in range(n_heads):
            for j in range(n_sel_pages):
                pid = pid_ref[(seq * n_heads + hh) * n_sel_pages + j]
                copies.append(pltpu.make_async_copy(ck_hbm.at[layer, pid, :, hh, :], kbuf.at[slot, hh, j],
                                                    sem.at[slot]))
                copies.append(pltpu.make_async_copy(cv_hbm.at[layer, pid, :, hh, :], vbuf.at[slot, hh, j],
                                                    sem.at[slot]))
        return copies

    slot = lax.rem(b, 2)

    @pl.when(b == 0)
    def _():
        for cp in page_copies(0, 0):
            cp.start()

    @pl.when(b + 1 < n_seq)
    def _():
        for cp in page_copies(b + 1, 1 - slot):
            cp.start()

    for cp in page_copies(b, slot):
        cp.wait()

    for hh in range(n_heads):
        sl = slice(hh * HEAD_DIM, (hh + 1) * HEAD_DIM)
        q = q_ref[0, :, sl]
        qb = jnp.broadcast_to(q, (8, HEAD_DIM)).astype(BF16)
        ks = kbuf[slot, hh].reshape(n_sel_pages * page, HEAD_DIM).astype(BF16)
        vs = vbuf[slot, hh].reshape(n_sel_pages * page, HEAD_DIM).astype(BF16)
        s = lax.dot_general(qb, ks, _NT, preferred_element_type=F32)[0:1, :] * scale
        far_bias = bf_ref[hh, 0:1, 0:1]
        bias = []
        for j in range(n_sel_pages):
            near = near_ref[(b * n_heads + hh) * (n_sel_pages // ppb) + j // ppb]
            bias.append(jnp.where(near == 1, bn_ref[hh, j % ppb:j % ppb + 1, :], far_bias))
        s = s + jnp.concatenate(bias, axis=1)
        kn = kn_ref[0, :, sl]
        s_own = jnp.sum(q.astype(BF16).astype(F32) * kn.astype(BF16).astype(F32), axis=-1, keepdims=True)
        s_own = s_own * scale + bf_ref[hh, 1:2, 0:1]
        m = jnp.maximum(jnp.max(s, axis=-1, keepdims=True), s_own)
        p = jnp.exp(s - m)
        p_own = jnp.exp(s_own - m)
        l = jnp.sum(p, axis=-1, keepdims=True) + p_own
        pv = jnp.dot(jnp.broadcast_to(p, (8, p.shape[1])).astype(BF16), vs, preferred_element_type=F32)[0:1, :]
        acc = pv + p_own.astype(BF16).astype(F32) * vn_ref[0, :, sl].astype(BF16).astype(F32)
        o_ref[0, hh] = (acc / l).astype(o_ref.dtype)


def _sample_attn(hs3, cache_k, cache_v, layer, sel_pages, near, bias_near, bias_far, *, n_sel_pages):
    bs = hs3.shape[0]
    _, _, page, n_heads, hd = cache_k.shape
    width = n_heads * hd
    hspec = lambda g: pl.BlockSpec((1, 1, width), lambda b, pid, nr: (b, 0, g))
    whole = lambda a: pl.BlockSpec(a.shape, lambda b, pid, nr: (0,) * a.ndim)
    buf = pltpu.VMEM((2, n_heads, n_sel_pages, page, hd), cache_k.dtype)
    return pl.pallas_call(
        functools.partial(_sample_attn_kernel, layer, n_sel_pages),
        grid_spec=pltpu.PrefetchScalarGridSpec(
            num_scalar_prefetch=2,
            grid=(bs,),
            in_specs=[hspec(0), hspec(1), hspec(2),
                      pl.BlockSpec(memory_space=pl.ANY), pl.BlockSpec(memory_space=pl.ANY),
                      whole(bias_near), whole(bias_far)],
            out_specs=pl.BlockSpec((1, n_heads, 1, hd), lambda b, pid, nr: (b, 0, 0, 0)),
            scratch_shapes=[buf, buf, pltpu.SemaphoreType.DMA((2,))],
        ),
        out_shape=jax.ShapeDtypeStruct((bs, n_heads, 1, hd), BF16),
        compiler_params=_cparams("arbitrary"),
        name="sample_attn",
    )(sel_pages, near, hs3, hs3, hs3, cache_k, cache_v, bias_near, bias_far)


def _ret_sample_kernel(q_ref, k_ref, v_ref, g_ref, cos_ref, sin_ref, dec_ref, s0_ref, o_ref, s_ref):
    n_heads = s_ref.shape[1]
    scale = HEAD_DIM ** -0.5
    cos = cos_ref[...]
    sin = sin_ref[...]
    rows = lax.broadcasted_iota(jnp.int32, (HEAD_DIM, HEAD_DIM), 0)
    cols = lax.broadcasted_iota(jnp.int32, (HEAD_DIM, HEAD_DIM), 1)
    eye = jnp.where(rows == cols, 1.0, 0.0)
    for hh in range(n_heads):
        sl = slice(hh * HEAD_DIM, (hh + 1) * HEAD_DIM)
        q = _rotate(q_ref[0, :, sl], cos, sin)
        k = _rotate(k_ref[0, :, sl], cos, sin) * scale
        v = v_ref[0, :, sl]
        g = g_ref[0, :, sl]
        dec = dec_ref[hh:hh + 1, :]
        q_col = jnp.sum(eye * q, axis=-1, keepdims=True)
        k_col = jnp.sum(eye * k, axis=-1, keepdims=True)
        s = s0_ref[0, hh]
        qk = jnp.sum(q * k, axis=-1, keepdims=True)
        o = qk * v + jnp.sum(q_col * s, axis=0, keepdims=True) * dec
        s_ref[0, hh] = s * dec + k_col * v
        o_ref[0, :, sl] = _head_norm_gate(o, g).astype(o_ref.dtype)


def _ret_sample(hs, cos, sin, decay, s0, *, n_heads, col0):
    bs = hs.shape[0]
    width = n_heads * HEAD_DIM
    hspec = lambda g: pl.BlockSpec((1, 1, width), lambda b: (b, 0, col0 + g))
    vspec = pl.BlockSpec((1, HEAD_DIM), lambda b: (0, 0))
    sspec = pl.BlockSpec((1, n_heads, HEAD_DIM, HEAD_DIM), lambda b: (b, 0, 0, 0))
    return pl.pallas_call(
        _ret_sample_kernel,
        grid=(bs,),
        in_specs=[hspec(0), hspec(1), hspec(2), hspec(3), vspec, vspec,
                  pl.BlockSpec((n_heads, HEAD_DIM), lambda b: (0, 0)), sspec],
        out_specs=[pl.BlockSpec((1, 1, width), lambda b: (b, 0, 0)), sspec],
        out_shape=[jax.ShapeDtypeStruct((bs, 1, width), BF16),
                   jax.ShapeDtypeStruct((bs, n_heads, HEAD_DIM, HEAD_DIM), F32)],
        compiler_params=_cparams("parallel"),
        name="ret_sample",
    )(hs, hs, hs, hs, cos, sin, decay, s0)


def _t5_bucket(rel):
    n = jnp.maximum(rel, 0)
    max_exact = NUM_BUCKETS // 2
    nf = jnp.maximum(n, max_exact).astype(F32)
    large = max_exact + (jnp.log(nf / max_exact) / math.log(MAX_DISTANCE / max_exact)
                         * (NUM_BUCKETS - max_exact)).astype(jnp.int32)
    return jnp.where(n < max_exact, n, jnp.minimum(large, NUM_BUCKETS - 1))


def _rope_tables(pos):
    half = HEAD_DIM // 2
    inv = ROPE_BASE ** (-jnp.arange(half, dtype=F32) / half)
    ang = pos[:, None] * inv[None, :]
    cos = jnp.cos(ang)
    sin = jnp.sin(ang)
    return jnp.concatenate([cos, cos], -1), jnp.concatenate([-sin, sin], -1)


def _decay_tables(n_heads, chunk):
    lg = jnp.log(1.0 - 2.0 ** (-5.0 - jnp.arange(n_heads, dtype=F32)))
    i = jnp.arange(chunk, dtype=F32)
    diff = i[:, None] - i[None, :]
    dmask = jnp.where(diff >= 0, jnp.exp(lg[:, None, None] * jnp.maximum(diff, 0.0)), 0.0)
    full = lambda col: jnp.broadcast_to(col[..., None], (n_heads, chunk, HEAD_DIM))
    cross = full(jnp.exp(lg[:, None] * (i + 1.0)))
    upd = full(jnp.exp(lg[:, None] * (chunk - 1.0 - i)))
    cdec = jnp.broadcast_to(jnp.exp(lg * chunk)[:, None, None], (n_heads, chunk, HEAD_DIM))
    return dmask, cross, upd, cdec


def kernel(x_prompt, x_sample, cache_k, cache_v, state_ret, state_conv, page_table, p_prompt, p_sample,
           rel_bias, w_in, w_out, ln1_g, ln1_b, w_gate, w_up, conv_w, conv_b, w_down, ln2_g, ln2_b,
           w_ple, w_ple_gate):
    bp, seq, d = x_prompt.shape
    bs, dec_seq, _ = x_sample.shape
    depth, n_pool, page, n_heads, hd = cache_k.shape
    n_pages = page_table.shape[1]
    past_len = n_pages * page
    nf = w_gate.shape[2]
    attn_w = n_heads * hd
    assert hd == HEAD_DIM and dec_seq == 1 and MOBA_BLOCK % page == 0
    assert seq % MOBA_BLOCK == 0 and past_len % MOBA_BLOCK == 0 and past_len // MOBA_BLOCK >= MOBA_TOPK
    assert w_in.shape[2] == 7 * attn_w and d == 2 * attn_w and RET_CHUNK == HEAD_DIM
    assert int(np.floor(np.log(np.float32(MOBA_BLOCK + 1) / 16) / math.log(MAX_DISTANCE / 16) * 16)) >= 15
    alpha = (2 * depth) ** 0.25
    ppb = MOBA_BLOCK // page
    n_past_blocks = past_len // MOBA_BLOCK
    mp = bp * seq

    w_pg_b = w_ple_gate.astype(BF16)
    vec3 = lambda a: a.reshape(depth, 1, a.shape[-1])
    ln1_g3, ln1_b3, ln2_g3, ln2_b3, conv_b3 = map(vec3, (ln1_g, ln1_b, ln2_g, ln2_b, conv_b))

    def bias_of(rel):
        onehot = (_t5_bucket(rel)[..., None] == jnp.arange(NUM_BUCKETS)).astype(F32)
        return jnp.einsum("...b,bh->h...", onehot, rel_bias.astype(F32), precision=lax.Precision.HIGHEST)

    ii = jnp.arange(MOBA_BLOCK, dtype=jnp.int32)
    far = rel_bias[NUM_BUCKETS - 1].astype(F32)
    bias_diag = bias_of(ii[:, None] - ii[None, :]) * LOG2E
    bias_adj = bias_of(MOBA_BLOCK + ii[:, None] - ii[None, :]) * LOG2E
    bias_far = jnp.broadcast_to((far * LOG2E)[:, None, None], (n_heads, 1, 128))
    bias_near_s = bias_of(MOBA_BLOCK - ii).reshape(n_heads, ppb, page)
    bias_far_s = jnp.stack([jnp.broadcast_to(far[:, None], (n_heads, 128)),
                            jnp.broadcast_to(rel_bias[0].astype(F32)[:, None], (n_heads, 128))], axis=1)

    cos_p, sin_p = _rope_tables(jnp.arange(seq, dtype=F32))
    cos_s, sin_s = _rope_tables(jnp.full((1,), float(past_len), F32))
    tabs = _decay_tables(n_heads, RET_CHUNK)
    decay_s = jnp.broadcast_to(
        (1.0 - 2.0 ** (-5.0 - jnp.arange(n_heads, dtype=F32)))[:, None], (n_heads, HEAD_DIM))
    decay_s = jnp.exp(jnp.log(decay_s))

    ck = cache_k.reshape(depth, n_pool, page * n_heads, hd)
    page_table_flat = page_table.reshape(-1)

    xp_f = x_prompt.reshape(mp, d)
    xp_b = xp_f.astype(BF16)
    xs_f = x_sample.reshape(bs, d)
    xs_b = xs_f.astype(BF16)
    s0_prompt = jnp.zeros((bp, n_heads, hd, hd), F32)
    conv0_prompt = jnp.zeros((bp, CONV_W - 1, nf), F32)

    pp3 = p_prompt.reshape(depth, mp, -1)
    ps3 = p_sample.reshape(depth, bs, -1)
    outs = {k: [] for k in ("ks", "vs", "rp", "rs", "cp", "cs")}
    kp_all = vp_all = None
    for l in range(depth):
        xs_pad = jnp.pad(xs_b, ((0, SAMPLE_ROWS_PAD - bs), (0, 0)))
        proj = functools.partial(_matmul, xp_b, xs_pad, w_in, l, tn=1024)
        tm_kv = 1024 if l else 512
        qp, qs = proj(col0=0, n=attn_w, tm=1024)
        kp_all, ks = proj(col0=attn_w, n=attn_w, tm=tm_kv, stack=kp_all, stacked=True)
        vp_all, vs = proj(col0=2 * attn_w, n=attn_w, tm=tm_kv, stack=vp_all, stacked=True)
        hr, rs = proj(col0=3 * attn_w, n=4 * attn_w, tm=1024)
        attn = _moba_prompt(qp, kp_all, vp_all, l, bias_diag, bias_adj, bias_far,
                            batch=bp, seq=seq, n_heads=n_heads)
        ret, s_new = _ret_prompt(hr, cos_p, sin_p, tabs, s0_prompt, batch=bp, seq=seq, n_heads=n_heads, col0=0)
        outs["rp"].append(s_new)
        x1_f, x1_b = _outproj_ln(attn, ret, w_out, l, xp_f, ln1_g3, ln1_b3, alpha=alpha, tm=512)
        gact, conv_new, ksum_l = _ffn_up_prompt(x1_b, w_gate, w_up, conv_w, conv_b3, conv0_prompt, l,
                                                page_table_flat, ck, batch=bp, seq=seq, tm=1024, tn=512,
                                                pages_per_block=ppb, n_heads=n_heads)
        ksum_l = ksum_l.reshape(bs, n_past_blocks, n_heads, hd)
        outs["cp"].append(conv_new)

        hs = jnp.concatenate([qs, ks, vs, rs], axis=1)[:bs]
        outs["ks"].append(hs[:, attn_w:2 * attn_w].reshape(bs, 1, n_heads, hd))
        outs["vs"].append(hs[:, 2 * attn_w:3 * attn_w].reshape(bs, 1, n_heads, hd))
        hs3 = hs.reshape(bs, 1, -1)
        picks = _sample_select(hs[:, :attn_w].reshape(bs, n_heads, hd), ksum_l)
        blocks = jnp.transpose(picks[..., 0], (0, 2, 1))
        sel_pos = blocks[..., None] * ppb + jnp.arange(ppb, dtype=jnp.int32)
        sel_pages = page_table[jnp.arange(bs)[:, None, None, None], sel_pos]
        near = (blocks == n_past_blocks - 1).astype(jnp.int32)
        attn_s = _sample_attn(hs3, cache_k, cache_v, l, sel_pages.reshape(-1), near.reshape(-1),
                              bias_near_s, bias_far_s, n_sel_pages=MOBA_TOPK * ppb)
        ret_s, s_new_s = _ret_sample(hs3, cos_s, sin_s, decay_s, state_ret[l], n_heads=n_heads, col0=3)
        outs["rs"].append(s_new_s)
        x1s_f, x1s_b = _outproj_ln(attn_s.reshape(bs, attn_w), ret_s.reshape(bs, attn_w), w_out, l, xs_f,
                                   ln1_g3, ln1_b3, alpha=alpha, tm=bs)
        gact_s, c0, c1 = _ffn_up_sample(x1s_b, w_gate, w_up, conv_w, conv_b3,
                                        state_conv[l, :, 0], state_conv[l, :, 1], l, tn=512)
        outs["cs"].append(jnp.stack([c0, c1], axis=1))

        gs_pad = jnp.pad(gact_s, ((0, SAMPLE_ROWS_PAD - bs), (0, 0)))
        f_p, f_s = _matmul(gact, gs_pad, w_down, l, col0=0, n=d, tm=512, tn=512)
        xp_f, xp_b = _ln_ple(f_p, x1_f, ln2_g3, ln2_b3, pp3, w_pg_b, w_ple, l, alpha=alpha, tm=512)
        xs_f, xs_b = _ln_ple(f_s[:bs], x1s_f, ln2_g3, ln2_b3, ps3, w_pg_b, w_ple, l, alpha=alpha, tm=bs)

    stack = lambda key: jnp.stack(outs[key])
    return (xp_f.reshape(bp, seq, d), xs_f.reshape(bs, 1, d),
            kp_all.reshape(depth, bp, seq, n_heads, hd), vp_all.reshape(depth, bp, seq, n_heads, hd),
            stack("ks"), stack("vs"), stack("rp"), stack("rs"), stack("cp"), stack("cs"))
```

```python
import functools
import math

import jax
import jax.numpy as jnp
import numpy as np
from jax import lax
from jax.experimental import pallas as pl
from jax.experimental.pallas import tpu as pltpu

HEAD_DIM = 128
MOBA_BLOCK = 256
MOBA_TOPK = 3
RET_CHUNK = 128
ROPE_BASE = 10000.0
NUM_BUCKETS = 32
MAX_DISTANCE = 128
CONV_W = 3
SAMPLE_ROWS_PAD = 16
LN_EPS = 1e-5

VMEM_LIMIT_BYTES = 56 * 1024 * 1024
NEG_BIG = -1e30
LOG2E = math.log2(math.e)

BF16 = jnp.bfloat16
F32 = jnp.float32

_NT = (((1,), (1,)), ((), ()))
_TN = (((0,), (0,)), ((), ()))


def _cparams(*sem):
    return pltpu.CompilerParams(dimension_semantics=sem, vmem_limit_bytes=VMEM_LIMIT_BYTES)


def _sigmoid(x):
    return 1.0 / (1.0 + jnp.exp(-x))


def _layer_norm_rows(y, g, b):
    mu = jnp.mean(y, axis=-1, keepdims=True)
    d = y - mu
    var = jnp.mean(d * d, axis=-1, keepdims=True)
    return d * lax.rsqrt(var + LN_EPS) * g + b


def _split_bf16(a):
    hi = a.astype(BF16)
    lo = (a - hi.astype(F32)).astype(BF16)
    return hi, lo


def _dot_nt_precise(a, b):
    ah, al = _split_bf16(a)
    bh, bl = _split_bf16(b)
    dot = functools.partial(lax.dot_general, dimension_numbers=_NT, preferred_element_type=F32)
    return dot(ah, bh) + (dot(ah, bl) + dot(al, bh))


def _mm_kernel(slab, x_ref, w_ref, xs_ref, *rest):
    o_ref, os_ref, wb_ref = rest[-3:]
    i = pl.program_id(1)

    @pl.when(i == 0)
    def _():
        wb_ref[...] = w_ref[0].astype(BF16)

    res = jnp.dot(x_ref[...], wb_ref[...], preferred_element_type=F32)
    if slab is None:
        o_ref[...] = res
    else:
        for other in range(o_ref.shape[0]):
            if other != slab:
                o_ref[other] = jnp.zeros(o_ref.shape[1:], o_ref.dtype)
        o_ref[slab] = res

    @pl.when(i == pl.num_programs(1) - 1)
    def _():
        os_ref[...] = jnp.dot(xs_ref[...], wb_ref[...], preferred_element_type=F32)


def _matmul(x, xs, w, layer, *, col0, n, tm, tn, stack=None, stacked=False):
    m, k = x.shape
    ms = xs.shape[0]
    depth = w.shape[0]
    assert col0 % tn == 0 and n % tn == 0
    in_specs = [pl.BlockSpec((tm, k), lambda j, i: (i, 0)),
                pl.BlockSpec((1, k, tn), lambda j, i: (layer, 0, col0 // tn + j)),
                pl.BlockSpec((ms, k), lambda j, i: (0, 0))]
    operands, aliases, slab = [x, w, xs], {}, None
    if not stacked:
        out_spec = pl.BlockSpec((tm, tn), lambda j, i: (i, j))
        out_shape = jax.ShapeDtypeStruct((m, n), F32)
    else:
        out_shape = jax.ShapeDtypeStruct((depth, m, n), F32)
        if stack is None:
            slab = layer
            out_spec = pl.BlockSpec((depth, tm, tn), lambda j, i: (0, i, j))
        else:
            in_specs.append(pl.BlockSpec(memory_space=pl.ANY))
            operands, aliases, slab = [x, w, xs, stack], {3: 0}, 0
            out_spec = pl.BlockSpec((1, tm, tn), lambda j, i: (layer, i, j))
    return pl.pallas_call(
        functools.partial(_mm_kernel, slab),
        grid=(n // tn, m // tm),
        in_specs=in_specs,
        out_specs=[out_spec, pl.BlockSpec((ms, tn), lambda j, i: (0, j))],
        out_shape=[out_shape, jax.ShapeDtypeStruct((ms, n), F32)],
        scratch_shapes=[pltpu.VMEM((k, tn), BF16)],
        input_output_aliases=aliases,
        compiler_params=_cparams("parallel", "arbitrary"),
        name="proj_in",
    )(*operands)


def _moba_select(cc, qf, km_ref):
    tq = qf.shape[0]
    nbp = km_ref.shape[0]
    gate = _dot_nt_precise(km_ref[...], qf)
    blk = lax.broadcasted_iota(jnp.int32, (nbp, tq), 0)
    eligible = blk < cc
    gate = jnp.where(eligible, gate, -jnp.inf)
    rank = jnp.zeros((nbp, tq), F32)
    for mth in range(cc):
        gm = gate[mth:mth + 1, :]
        beats = jnp.logical_or(gm > gate, jnp.logical_and(gm == gate, mth < blk))
        rank = rank + jnp.where(beats, 1.0, 0.0)
    sel_t = jnp.where(jnp.logical_and(eligible, rank < MOBA_TOPK), 1.0, 0.0)
    sel_pad = jnp.concatenate([sel_t, jnp.zeros((128 - nbp, tq), F32)], axis=0).astype(BF16)
    rows = lax.broadcasted_iota(jnp.int32, (tq, tq), 0)
    cols = lax.broadcasted_iota(jnp.int32, (tq, tq), 1)
    eye = jnp.where(rows == cols, 1.0, 0.0).astype(BF16)
    return lax.dot_general(eye, sel_pad, _NT, preferred_element_type=F32)


def _moba_tile(cc, q_ref, bd_ref, ba_ref, bf_ref, o_ref, kb_ref, vb_ref, km_ref):
    tq = q_ref.shape[0]
    scale2 = HEAD_DIM ** -0.5 * LOG2E
    nk = (cc + 1) * MOBA_BLOCK
    rows = lax.broadcasted_iota(jnp.int32, (tq, MOBA_BLOCK), 0)
    cols = lax.broadcasted_iota(jnp.int32, (tq, MOBA_BLOCK), 1)
    for hp in range(q_ref.shape[1] // HEAD_DIM):
        sl = slice(hp * HEAD_DIM, (hp + 1) * HEAD_DIM)
        qf = q_ref[:, sl]
        s = lax.dot_general(qf.astype(BF16), kb_ref[0:nk, sl], _NT, preferred_element_type=F32) * scale2
        sel = _moba_select(cc, qf, km_ref.at[hp]) if cc > MOBA_TOPK else None
        far_bias = bf_ref[hp, 0:1, 0:1]
        pieces = []
        for n in range(cc + 1):
            sn = s[:, n * MOBA_BLOCK:(n + 1) * MOBA_BLOCK]
            if n == cc:
                sn = jnp.where(cols <= rows, sn + bd_ref[hp], NEG_BIG)
            elif n == cc - 1:
                sn = sn + ba_ref[hp]
            else:
                sn = sn + far_bias
            if n < cc and sel is not None:
                sn = jnp.where(sel[:, n:n + 1] > 0.5, sn, NEG_BIG)
            pieces.append(sn)
        s = jnp.concatenate(pieces, axis=1) if cc else pieces[0]
        m = jnp.max(s, axis=-1, keepdims=True)
        p = jnp.exp2(s - m)
        l = jnp.sum(p, axis=-1, keepdims=True)
        acc = jnp.dot(p.astype(BF16), vb_ref[0:nk, sl], preferred_element_type=F32)
        o_ref[:, sl] = (acc / l).astype(o_ref.dtype)


def _moba_prompt_kernel(q_ref, k_ref, v_ref, bd_ref, ba_ref, bf_ref, o_ref, kb_ref, vb_ref, km_ref):
    c = pl.program_id(2)
    nb = k_ref.shape[1] // MOBA_BLOCK

    @pl.when(c == 0)
    def _():
        kb_ref[...] = k_ref[0].astype(BF16)
        vb_ref[...] = v_ref[0].astype(BF16)
        km_ref[...] = jnp.zeros_like(km_ref)
        for hp in range(km_ref.shape[0]):
            sl = slice(hp * HEAD_DIM, (hp + 1) * HEAD_DIM)
            for n in range(nb):
                km_ref[hp, n:n + 1, :] = jnp.mean(k_ref[0, n * MOBA_BLOCK:(n + 1) * MOBA_BLOCK, sl],
                                                  axis=0, keepdims=True)

    for cc in range(nb):
        pl.when(c == cc)(functools.partial(_moba_tile, cc, q_ref, bd_ref, ba_ref, bf_ref, o_ref,
                                           kb_ref, vb_ref, km_ref))


MOBA_HEADS_PER_STEP = 4


def _moba_prompt(q, k, v, layer, bias_diag, bias_adj, bias_far, *, batch, seq, n_heads):
    nq = seq // MOBA_BLOCK
    nbias = bias_far.shape[1]
    hps = MOBA_HEADS_PER_STEP
    width = hps * HEAD_DIM
    kvspec = pl.BlockSpec((1, seq, width), lambda b, hh, qi: (layer, b, hh))
    bspec = pl.BlockSpec((hps, MOBA_BLOCK, MOBA_BLOCK), lambda b, hh, qi: (hh, 0, 0))
    return pl.pallas_call(
        _moba_prompt_kernel,
        grid=(batch, n_heads // hps, nq),
        in_specs=[
            pl.BlockSpec((MOBA_BLOCK, width), lambda b, hh, qi: (b * nq + qi, hh)),
            kvspec, kvspec, bspec, bspec,
            pl.BlockSpec((hps, nbias, 128), lambda b, hh, qi: (hh, 0, 0)),
        ],
        out_specs=pl.BlockSpec((MOBA_BLOCK, width), lambda b, hh, qi: (b * nq + qi, hh)),
        out_shape=jax.ShapeDtypeStruct((batch * seq, n_heads * HEAD_DIM), BF16),
        scratch_shapes=[
            pltpu.VMEM((seq, width), BF16),
            pltpu.VMEM((seq, width), BF16),
            pltpu.VMEM((hps, max(16, seq // MOBA_BLOCK), HEAD_DIM), F32),
        ],
        compiler_params=_cparams("parallel", "parallel", "arbitrary"),
        name="moba_prompt",
    )(q, k, v, bias_diag, bias_adj, bias_far)


def _rotate(x, cos, sin_signed):
    return x * cos + pltpu.roll(x, HEAD_DIM // 2, 1) * sin_signed


def _head_norm_gate(o, g):
    mu = jnp.mean(o, axis=-1, keepdims=True)
    d = o - mu
    var = jnp.mean(d * d, axis=-1, keepdims=True)
    return g * _sigmoid(g) * (d * lax.rsqrt(var + LN_EPS))


def _ret_prompt_kernel(q_ref, k_ref, v_ref, g_ref, cos_ref, sin_ref, dm_ref, cr_ref, up_ref, cd_ref,
                       s0_ref, o_ref, s_ref):
    n_heads = s_ref.shape[1]
    scale = HEAD_DIM ** -0.5

    @pl.when(pl.program_id(1) == 0)
    def _():
        s_ref[...] = s0_ref[...]

    cos = cos_ref[...]
    sin = sin_ref[...]
    for hh in range(n_heads):
        sl = slice(hh * HEAD_DIM, (hh + 1) * HEAD_DIM)
        q = _rotate(q_ref[:, sl], cos, sin)
        k = _rotate(k_ref[:, sl], cos, sin) * scale
        qb = q.astype(BF16)
        kb = k.astype(BF16)
        vb = v_ref[:, sl].astype(BF16)
        s = s_ref[0, hh]
        sc = lax.dot_general(qb, kb, _NT, preferred_element_type=F32) * dm_ref[hh]
        o = jnp.dot(sc.astype(BF16), vb, preferred_element_type=F32)
        o = o + jnp.dot(qb, s.astype(BF16), preferred_element_type=F32) * cr_ref[hh]
        ku = (k * up_ref[hh]).astype(BF16)
        s_ref[0, hh] = s * cd_ref[hh] + lax.dot_general(ku, vb, _TN, preferred_element_type=F32)
        o_ref[:, sl] = _head_norm_gate(o, g_ref[:, sl]).astype(o_ref.dtype)


def _ret_prompt(h, cos, sin, tabs, s0, *, batch, seq, n_heads, col0):
    nc = seq // RET_CHUNK
    width = n_heads * HEAD_DIM
    hspec = lambda g: pl.BlockSpec((RET_CHUNK, width), lambda b, c: (b * nc + c, col0 + g))
    tspec = pl.BlockSpec((n_heads, RET_CHUNK, HEAD_DIM), lambda b, c: (0, 0, 0))
    sspec = pl.BlockSpec((1, n_heads, HEAD_DIM, HEAD_DIM), lambda b, c: (b, 0, 0, 0))
    return pl.pallas_call(
        _ret_prompt_kernel,
        grid=(batch, nc),
        in_specs=[hspec(0), hspec(1), hspec(2), hspec(3),
                  pl.BlockSpec((RET_CHUNK, HEAD_DIM), lambda b, c: (c, 0)),
                  pl.BlockSpec((RET_CHUNK, HEAD_DIM), lambda b, c: (c, 0)),
                  tspec, tspec, tspec, tspec, sspec],
        out_specs=[pl.BlockSpec((RET_CHUNK, width), lambda b, c: (b * nc + c, 0)), sspec],
        out_shape=[jax.ShapeDtypeStruct((batch * seq, width), BF16),
                   jax.ShapeDtypeStruct((batch, n_heads, HEAD_DIM, HEAD_DIM), F32)],
        compiler_params=_cparams("parallel", "arbitrary"),
        name="ret_prompt",
    )(h, h, h, h, cos, sin, *tabs, s0)


def _outproj_kernel(alpha, a_ref, r_ref, wa_ref, wr_ref, x_ref, g_ref, b_ref, of_ref, ob_ref, wb_ref):
    @pl.when(pl.program_id(0) == 0)
    def _():
        wb_ref[0] = wa_ref[0].astype(BF16)
        wb_ref[1] = wr_ref[0].astype(BF16)

    mix = jnp.dot(a_ref[...], wb_ref[0], preferred_element_type=F32)
    mix = mix + jnp.dot(r_ref[...], wb_ref[1], preferred_element_type=F32)
    y = _layer_norm_rows(alpha * x_ref[...] + mix, g_ref[0], b_ref[0])
    of_ref[...] = y
    ob_ref[...] = y.astype(BF16)


def _outproj_ln(attn, ret, w_out, layer, x, g, b, *, alpha, tm):
    m, ka = attn.shape
    d = x.shape[1]
    row = lambda width: pl.BlockSpec((tm, width), lambda i: (i, 0))
    vec = pl.BlockSpec((1, 1, d), lambda i: (layer, 0, 0))
    return pl.pallas_call(
        functools.partial(_outproj_kernel, alpha),
        grid=(m // tm,),
        in_specs=[row(ka), row(ka),
                  pl.BlockSpec((1, ka, d), lambda i: (layer, 0, 0), pipeline_mode=pl.Buffered(1)),
                  pl.BlockSpec((1, ka, d), lambda i: (layer, 1, 0), pipeline_mode=pl.Buffered(1)),
                  row(d), vec, vec],
        out_specs=[row(d), row(d)],
        out_shape=[jax.ShapeDtypeStruct((m, d), F32), jax.ShapeDtypeStruct((m, d), BF16)],
        scratch_shapes=[pltpu.VMEM((2, ka, d), BF16)],
        compiler_params=_cparams("arbitrary"),
        name="outproj_ln",
    )(attn, ret, w_out, w_out, x, g, b)


KSUM_PAGES_PER_STEP = 16


def _ffn_up_prompt_kernel(tiles_per_seq, n_pages, ids_ref, x_ref, wg_ref, wu_ref, cw_ref, cb_ref, buf_ref,
                          *rest):
    del ids_ref
    page_refs = rest[:n_pages]
    o_ref, cn_ref, ks_ref, carry_ref, wgb_ref, wub_ref = rest[n_pages:]
    i = pl.program_id(1)
    tm = x_ref.shape[0]

    @pl.when(i == 0)
    def _():
        wgb_ref[...] = wg_ref[0].astype(BF16)
        wub_ref[...] = wu_ref[0].astype(BF16)

    x = x_ref[...]
    a = jnp.dot(x, wgb_ref[...], preferred_element_type=F32)
    u = jnp.dot(x, wub_ref[...], preferred_element_type=F32)
    first = (i % tiles_per_seq) == 0
    prev = jnp.where(first, buf_ref[0], carry_ref[6:8, :])
    row = lax.broadcasted_iota(jnp.int32, a.shape, 0)
    a1 = jnp.where(row == 0, prev[1:2, :], pltpu.roll(a, 1, 0))
    a2 = jnp.where(row == 0, prev[0:1, :], jnp.where(row == 1, prev[1:2, :], pltpu.roll(a, 2, 0)))
    cw = cw_ref[0]
    conv = cb_ref[0] + cw[0:1, :] * a2 + cw[1:2, :] * a1 + cw[2:3, :] * a
    o_ref[...] = (conv * _sigmoid(conv) * u).astype(o_ref.dtype)
    carry_ref[...] = a[tm - 8:tm, :]

    @pl.when((i % tiles_per_seq) == tiles_per_seq - 1)
    def _():
        cn_ref[0] = a[tm - (CONV_W - 1):tm, :]

    n_heads = ks_ref.shape[1]
    ppb = n_pages // ks_ref.shape[0]
    for r, p_ref in enumerate(page_refs):
        pg = p_ref[0, 0]
        part = jnp.sum(pg.reshape(pg.shape[0] // n_heads, n_heads, pg.shape[1]), axis=0)
        if r % ppb == 0:
            ks_ref[r // ppb] = part
        else:
            ks_ref[r // ppb] += part


def _ffn_up_prompt(x, w_gate, w_up, conv_w, conv_b, conv_buf, layer, page_table_flat, cache_k_rows, *,
                   batch, seq, tm, tn, pages_per_block, n_heads):
    m, d = x.shape
    nf = w_gate.shape[2]
    tps = seq // tm
    n_i = m // tm
    _, _, rows, hd = cache_k_rows.shape
    pps = KSUM_PAGES_PER_STEP
    n_pages = page_table_flat.shape[0]
    n_groups = n_pages // pps
    bps = pps // pages_per_block
    assert n_pages % pps == 0 and pps % pages_per_block == 0 and n_groups <= (nf // tn) * n_i
    group = lambda j, i: jnp.minimum(j * n_i + i, n_groups - 1)
    page_spec = lambda r: pl.BlockSpec(
        (1, 1, rows, hd), lambda j, i, ids: (layer, ids[pps * group(j, i) + r], 0, 0))
    wspec = pl.BlockSpec((1, d, tn), lambda j, i, ids: (layer, 0, j))
    return pl.pallas_call(
        functools.partial(_ffn_up_prompt_kernel, tps, pps),
        grid_spec=pltpu.PrefetchScalarGridSpec(
            num_scalar_prefetch=1,
            grid=(nf // tn, n_i),
            in_specs=[pl.BlockSpec((tm, d), lambda j, i, ids: (i, 0)), wspec, wspec,
                      pl.BlockSpec((1, CONV_W, tn), lambda j, i, ids: (layer, 0, j)),
                      pl.BlockSpec((1, 1, tn), lambda j, i, ids: (layer, 0, j)),
                      pl.BlockSpec((1, CONV_W - 1, tn), lambda j, i, ids: (i // tps, 0, j))]
            + [page_spec(r) for r in range(pps)],
            out_specs=[pl.BlockSpec((tm, tn), lambda j, i, ids: (i, j)),
                       pl.BlockSpec((1, CONV_W - 1, tn), lambda j, i, ids: (i // tps, 0, j)),
                       pl.BlockSpec((bps, n_heads, hd), lambda j, i, ids: (group(j, i), 0, 0))],
            scratch_shapes=[pltpu.VMEM((8, tn), F32), pltpu.VMEM((d, tn), BF16), pltpu.VMEM((d, tn), BF16)],
        ),
        out_shape=[jax.ShapeDtypeStruct((m, nf), BF16),
                   jax.ShapeDtypeStruct((batch, CONV_W - 1, nf), F32),
                   jax.ShapeDtypeStruct((n_pages // pages_per_block, n_heads, hd), F32)],
        compiler_params=_cparams("arbitrary", "arbitrary"),
        name="ffn_up_prompt",
    )(page_table_flat, x, w_gate, w_up, conv_w, conv_b, conv_buf, *([cache_k_rows] * pps))


def _ffn_up_sample_kernel(x_ref, wg_ref, wu_ref, cw_ref, cb_ref, b0_ref, b1_ref, o_ref, n0_ref, n1_ref):
    x = x_ref[...]
    a = jnp.dot(x, wg_ref[0].astype(BF16), preferred_element_type=F32)
    u = jnp.dot(x, wu_ref[0].astype(BF16), preferred_element_type=F32)
    cw = cw_ref[0]
    conv = cb_ref[0] + cw[0:1, :] * b0_ref[...] + cw[1:2, :] * b1_ref[...] + cw[2:3, :] * a
    o_ref[...] = (conv * _sigmoid(conv) * u).astype(o_ref.dtype)
    n0_ref[...] = b1_ref[...]
    n1_ref[...] = a


def _ffn_up_sample(x, w_gate, w_up, conv_w, conv_b, buf0, buf1, layer, *, tn):
    m, d = x.shape
    nf = w_gate.shape[2]
    wspec = pl.BlockSpec((1, d, tn), lambda j: (layer, 0, j))
    cspec = pl.BlockSpec((m, tn), lambda j: (0, j))
    return pl.pallas_call(
        _ffn_up_sample_kernel,
        grid=(nf // tn,),
        in_specs=[pl.BlockSpec((m, d), lambda j: (0, 0)), wspec, wspec,
                  pl.BlockSpec((1, CONV_W, tn), lambda j: (layer, 0, j)),
                  pl.BlockSpec((1, 1, tn), lambda j: (layer, 0, j)),
                  cspec, cspec],
        out_specs=[cspec, cspec, cspec],
        out_shape=[jax.ShapeDtypeStruct((m, nf), BF16),
                   jax.ShapeDtypeStruct((m, nf), F32),
                   jax.ShapeDtypeStruct((m, nf), F32)],
        compiler_params=_cparams("parallel"),
        name="ffn_up_sample",
    )(x, w_gate, w_up, conv_w, conv_b, buf0, buf1)


def _ln_ple_kernel(alpha, f_ref, x_ref, lg_ref, lb_ref, p_ref, wg_ref, wp_ref, of_ref, ob_ref):
    x2 = _layer_norm_rows(alpha * x_ref[...] + f_ref[...], lg_ref[0], lb_ref[0])
    gate = _sigmoid(jnp.dot(x2.astype(BF16), wg_ref[0], preferred_element_type=F32))
    emb = jnp.dot(p_ref[0].astype(BF16), wp_ref[0].astype(BF16), preferred_element_type=F32)
    y = x2 + gate * emb
    of_ref[...] = y
    ob_ref[...] = y.astype(BF16)


def _ln_ple(f, x, ln_g, ln_b, p, w_ple_gate_b, w_ple, layer, *, alpha, tm):
    m, d = x.shape
    pd = p.shape[2]
    row = lambda width: pl.BlockSpec((tm, width), lambda i: (i, 0))
    vec = pl.BlockSpec((1, 1, d), lambda i: (layer, 0, 0))
    return pl.pallas_call(
        functools.partial(_ln_ple_kernel, alpha),
        grid=(m // tm,),
        in_specs=[row(d), row(d), vec, vec, pl.BlockSpec((1, tm, pd), lambda i: (layer, i, 0)),
                  pl.BlockSpec((1, d, d), lambda i: (layer, 0, 0), pipeline_mode=pl.Buffered(1)),
                  pl.BlockSpec((1, pd, d), lambda i: (layer, 0, 0))],
        out_specs=[row(d), row(d)],
        out_shape=[jax.ShapeDtypeStruct((m, d), F32), jax.ShapeDtypeStruct((m, d), BF16)],
        compiler_params=_cparams("parallel"),
        name="ln_ple",
    )(f, x, ln_g, ln_b, p, w_ple_gate_b, w_ple)


def _sample_select_kernel(q_ref, ks_ref, o_ref):
    nblk, n_heads = ks_ref.shape[1], ks_ref.shape[2]
    prod = ks_ref[0] * q_ref[0][None] * (1.0 / MOBA_BLOCK)
    gate = jnp.sum(prod, axis=-1, keepdims=True)
    blk = lax.broadcasted_iota(jnp.int32, gate.shape, 0).astype(F32)
    for t in range(MOBA_TOPK):
        best = jnp.max(gate, axis=0, keepdims=True)
        idx = jnp.min(jnp.where(gate == best, blk, float(nblk)), axis=0, keepdims=True)
        o_ref[0, t] = jnp.broadcast_to(idx[0], (n_heads, 128)).astype(jnp.int32)
        gate = jnp.where(blk == idx, -jnp.inf, gate)


def _sample_select(q_heads, ksum_l):
    bs, nblk, n_heads, hd = ksum_l.shape
    return pl.pallas_call(
        _sample_select_kernel,
        grid=(bs,),
        in_specs=[pl.BlockSpec((1, n_heads, hd), lambda b: (b, 0, 0)),
                  pl.BlockSpec((1, nblk, n_heads, hd), lambda b: (b, 0, 0, 0))],
        out_specs=pl.BlockSpec((1, MOBA_TOPK, n_heads, 128), lambda b: (b, 0, 0, 0)),
        out_shape=jax.ShapeDtypeStruct((bs, MOBA_TOPK, n_heads, 128), jnp.int32),
        compiler_params=_cparams("parallel"),
        name="sample_select",
    )(q_heads, ksum_l)


def _sample_attn_kernel(layer, n_sel_pages, pid_ref, near_ref, q_ref, kn_ref, vn_ref, ck_hbm, cv_hbm,
                        bn_ref, bf_ref, o_ref, kbuf, vbuf, sem):
    b = pl.program_id(0)
    n_seq = pl.num_programs(0)
    n_heads = o_ref.shape[1]
    ppb, page = bn_ref.shape[1], bn_ref.shape[2]
    scale = HEAD_DIM ** -0.5

    def page_copies(seq, slot):
        copies = []
        for hh in range(n_heads):
            for j in range(n_sel_pages):
                pid = pid_ref[(seq * n_heads + hh) * n_sel_pages + j]
                copies.append(pltpu.make_async_copy(ck_hbm.at[layer, pid, :, hh, :], kbuf.at[slot, hh, j],
                                                    sem.at[slot]))
                copies.append(pltpu.make_async_copy(cv_hbm.at[layer, pid, :, hh, :], vbuf.at[slot, hh, j],
                                                    sem.at[slot]))
        return copies

    slot = lax.rem(b, 2)

    @pl.when(b == 0)
    def _():
        for cp in page_copies(0, 0):
            cp.start()

    @pl.when(b + 1 < n_seq)
    def _():
        for cp in page_copies(b + 1, 1 - slot):
            cp.start()

    for cp in page_copies(b, slot):
        cp.wait()

    for hh in range(n_heads):
        sl = slice(hh * HEAD_DIM, (hh + 1) * HEAD_DIM)
        q = q_ref[0, :, sl]
        qb = jnp.broadcast_to(q, (8, HEAD_DIM)).astype(BF16)
        ks = kbuf[slot, hh].reshape(n_sel_pages * page, HEAD_DIM).astype(BF16)
        vs = vbuf[slot, hh].reshape(n_sel_pages * page, HEAD_DIM).astype(BF16)
        s = lax.dot_general(qb, ks, _NT, preferred_element_type=F32)[0:1, :] * scale
        far_bias = bf_ref[hh, 0:1, 0:1]
        bias = []
        for j in range(n_sel_pages):
            near = near_ref[(b * n_heads + hh) * (n_sel_pages // ppb) + j // ppb]
            bias.append(jnp.where(near == 1, bn_ref[hh, j % ppb:j % ppb + 1, :], far_bias))
        s = s + jnp.concatenate(bias, axis=1)
        kn = kn_ref[0, :, sl]
        s_own = jnp.sum(q.astype(BF16).astype(F32) * kn.astype(BF16).astype(F32), axis=-1, keepdims=True)
        s_own = s_own * scale + bf_ref[hh, 1:2, 0:1]
        m = jnp.maximum(jnp.max(s, axis=-1, keepdims=True), s_own)
        p = jnp.exp(s - m)
        p_own = jnp.exp(s_own - m)
        l = jnp.sum(p, axis=-1, keepdims=True) + p_own
        pv = jnp.dot(jnp.broadcast_to(p, (8, p.shape[1])).astype(BF16), vs, preferred_element_type=F32)[0:1, :]
        acc = pv + p_own.astype(BF16).astype(F32) * vn_ref[0, :, sl].astype(BF16).astype(F32)
        o_ref[0, hh] = (acc / l).astype(o_ref.dtype)


def _sample_attn(hs3, cache_k, cache_v, layer, sel_pages, near, bias_near, bias_far, *, n_sel_pages):
    bs = hs3.shape[0]
    _, _, page, n_heads, hd = cache_k.shape
    width = n_heads * hd
    hspec = lambda g: pl.BlockSpec((1, 1, width), lambda b, pid, nr: (b, 0, g))
    whole = lambda a: pl.BlockSpec(a.shape, lambda b, pid, nr: (0,) * a.ndim)
    buf = pltpu.VMEM((2, n_heads, n_sel_pages, page, hd), cache_k.dtype)
    return pl.pallas_call(
        functools.partial(_sample_attn_kernel, layer, n_sel_pages),
        grid_spec=pltpu.PrefetchScalarGridSpec(
            num_scalar_prefetch=2,
            grid=(bs,),
            in_specs=[hspec(0), hspec(1), hspec(2),
                      pl.BlockSpec(memory_space=pl.ANY), pl.BlockSpec(memory_space=pl.ANY),
                      whole(bias_near), whole(bias_far)],
            out_specs=pl.BlockSpec((1, n_heads, 1, hd), lambda b, pid, nr: (b, 0, 0, 0)),
            scratch_shapes=[buf, buf, pltpu.SemaphoreType.DMA((2,))],
        ),
        out_shape=jax.ShapeDtypeStruct((bs, n_heads, 1, hd), BF16),
        compiler_params=_cparams("arbitrary"),
        name="sample_attn",
    )(sel_pages, near, hs3, hs3, hs3, cache_k, cache_v, bias_near, bias_far)


def _ret_sample_kernel(q_ref, k_ref, v_ref, g_ref, cos_ref, sin_ref, dec_ref, s0_ref, o_ref, s_ref):
    n_heads = s_ref.shape[1]
    scale = HEAD_DIM ** -0.5
    cos = cos_ref[...]
    sin = sin_ref[...]
    rows = lax.broadcasted_iota(jnp.int32, (HEAD_DIM, HEAD_DIM), 0)
    cols = lax.broadcasted_iota(jnp.int32, (HEAD_DIM, HEAD_DIM), 1)
    eye = jnp.where(rows == cols, 1.0, 0.0)
    for hh in range(n_heads):
        sl = slice(hh * HEAD_DIM, (hh + 1) * HEAD_DIM)
        q = _rotate(q_ref[0, :, sl], cos, sin)
        k = _rotate(k_ref[0, :, sl], cos, sin) * scale
        v = v_ref[0, :, sl]
        g = g_ref[0, :, sl]
        dec = dec_ref[hh:hh + 1, :]
        q_col = jnp.sum(eye * q, axis=-1, keepdims=True)
        k_col = jnp.sum(eye * k, axis=-1, keepdims=True)
        s = s0_ref[0, hh]
        qk = jnp.sum(q * k, axis=-1, keepdims=True)
        o = qk * v + jnp.sum(q_col * s, axis=0, keepdims=True) * dec
        s_ref[0, hh] = s * dec + k_col * v
        o_ref[0, :, sl] = _head_norm_gate(o, g).astype(o_ref.dtype)


def _ret_sample(hs, cos, sin, decay, s0, *, n_heads, col0):
    bs = hs.shape[0]
    width = n_heads * HEAD_DIM
    hspec = lambda g: pl.BlockSpec((1, 1, width), lambda b: (b, 0, col0 + g))
    vspec = pl.BlockSpec((1, HEAD_DIM), lambda b: (0, 0))
    sspec = pl.BlockSpec((1, n_heads, HEAD_DIM, HEAD_DIM), lambda b: (b, 0, 0, 0))
    return pl.pallas_call(
        _ret_sample_kernel,
        grid=(bs,),
        in_specs=[hspec(0), hspec(1), hspec(2), hspec(3), vspec, vspec,
                  pl.BlockSpec((n_heads, HEAD_DIM), lambda b: (0, 0)), sspec],
        out_specs=[pl.BlockSpec((1, 1, width), lambda b: (b, 0, 0)), sspec],
        out_shape=[jax.ShapeDtypeStruct((bs, 1, width), BF16),
                   jax.ShapeDtypeStruct((bs, n_heads, HEAD_DIM, HEAD_DIM), F32)],
        compiler_params=_cparams("parallel"),
        name="ret_sample",
    )(hs, hs, hs, hs, cos, sin, decay, s0)


def _t5_bucket(rel):
    n = jnp.maximum(rel, 0)
    max_exact = NUM_BUCKETS // 2
    nf = jnp.maximum(n, max_exact).astype(F32)
    large = max_exact + (jnp.log(nf / max_exact) / math.log(MAX_DISTANCE / max_exact)
                         * (NUM_BUCKETS - max_exact)).astype(jnp.int32)
    return jnp.where(n < max_exact, n, jnp.minimum(large, NUM_BUCKETS - 1))


def _rope_tables(pos):
    half = HEAD_DIM // 2
    inv = ROPE_BASE ** (-jnp.arange(half, dtype=F32) / half)
    ang = pos[:, None] * inv[None, :]
    cos = jnp.cos(ang)
    sin = jnp.sin(ang)
    return jnp.concatenate([cos, cos], -1), jnp.concatenate([-sin, sin], -1)


def _decay_tables(n_heads, chunk):
    lg = jnp.log(1.0 - 2.0 ** (-5.0 - jnp.arange(n_heads, dtype=F32)))
    i = jnp.arange(chunk, dtype=F32)
    diff = i[:, None] - i[None, :]
    dmask = jnp.where(diff >= 0, jnp.exp(lg[:, None, None] * jnp.maximum(diff, 0.0)), 0.0)
    full = lambda col: jnp.broadcast_to(col[..., None], (n_heads, chunk, HEAD_DIM))
    cross = full(jnp.exp(lg[:, None] * (i + 1.0)))
    upd = full(jnp.exp(lg[:, None] * (chunk - 1.0 - i)))
    cdec = jnp.broadcast_to(jnp.exp(lg * chunk)[:, None, None], (n_heads, chunk, HEAD_DIM))
    return dmask, cross, upd, cdec


def kernel(x_prompt, x_sample, cache_k, cache_v, state_ret, state_conv, page_table, p_prompt, p_sample,
           rel_bias, w_in, w_out, ln1_g, ln1_b, w_gate, w_up, conv_w, conv_b, w_down, ln2_g, ln2_b,
           w_ple, w_ple_gate):
    bp, seq, d = x_prompt.shape
    bs, dec_seq, _ = x_sample.shape
    depth, n_pool, page, n_heads, hd = cache_k.shape
    n_pages = page_table.shape[1]
    past_len = n_pages * page
    nf = w_gate.shape[2]
    attn_w = n_heads * hd
    assert hd == HEAD_DIM and dec_seq == 1 and MOBA_BLOCK % page == 0
    assert seq % MOBA_BLOCK == 0 and past_len % MOBA_BLOCK == 0 and past_len // MOBA_BLOCK >= MOBA_TOPK
    assert w_in.shape[2] == 7 * attn_w and d == 2 * attn_w and RET_CHUNK == HEAD_DIM
    assert int(np.floor(np.log(np.float32(MOBA_BLOCK + 1) / 16) / math.log(MAX_DISTANCE / 16) * 16)) >= 15
    alpha = (2 * depth) ** 0.25
    ppb = MOBA_BLOCK // page
    n_past_blocks = past_len // MOBA_BLOCK
    mp = bp * seq

    w_pg_b = w_ple_gate.astype(BF16)
    vec3 = lambda a: a.reshape(depth, 1, a.shape[-1])
    ln1_g3, ln1_b3, ln2_g3, ln2_b3, conv_b3 = map(vec3, (ln1_g, ln1_b, ln2_g, ln2_b, conv_b))

    def bias_of(rel):
        onehot = (_t5_bucket(rel)[..., None] == jnp.arange(NUM_BUCKETS)).astype(F32)
        return jnp.einsum("...b,bh->h...", onehot, rel_bias.astype(F32), precision=lax.Precision.HIGHEST)

    ii = jnp.arange(MOBA_BLOCK, dtype=jnp.int32)
    far = rel_bias[NUM_BUCKETS - 1].astype(F32)
    bias_diag = bias_of(ii[:, None] - ii[None, :]) * LOG2E
    bias_adj = bias_of(MOBA_BLOCK + ii[:, None] - ii[None, :]) * LOG2E
    bias_far = jnp.broadcast_to((far * LOG2E)[:, None, None], (n_heads, 1, 128))
    bias_near_s = bias_of(MOBA_BLOCK - ii).reshape(n_heads, ppb, page)
    bias_far_s = jnp.stack([jnp.broadcast_to(far[:, None], (n_heads, 128)),
                            jnp.broadcast_to(rel_bias[0].astype(F32)[:, None], (n_heads, 128))], axis=1)

    cos_p, sin_p = _rope_tables(jnp.arange(seq, dtype=F32))
    cos_s, sin_s = _rope_tables(jnp.full((1,), float(past_len), F32))
    tabs = _decay_tables(n_heads, RET_CHUNK)
    decay_s = jnp.broadcast_to(
        (1.0 - 2.0 ** (-5.0 - jnp.arange(n_heads, dtype=F32)))[:, None], (n_heads, HEAD_DIM))
    decay_s = jnp.exp(jnp.log(decay_s))

    ck = cache_k.reshape(depth, n_pool, page * n_heads, hd)
    page_table_flat = page_table.reshape(-1)

    xp_f = x_prompt.reshape(mp, d)
    xp_b = xp_f.astype(BF16)
    xs_f = x_sample.reshape(bs, d)
    xs_b = xs_f.astype(BF16)
    s0_prompt = jnp.zeros((bp, n_heads, hd, hd), F32)
    conv0_prompt = jnp.zeros((bp, CONV_W - 1, nf), F32)

    pp3 = p_prompt.reshape(depth, mp, -1)
    ps3 = p_sample.reshape(depth, bs, -1)
    outs = {k: [] for k in ("ks", "vs", "rp", "rs", "cp", "cs")}
    kp_all = vp_all = None
    for l in range(depth):
        xs_pad = jnp.pad(xs_b, ((0, SAMPLE_ROWS_PAD - bs), (0, 0)))
        proj = functools.partial(_matmul, xp_b, xs_pad, w_in, l, tn=1024)
        tm_kv = 1024 if l else 512
        qp, qs = proj(col0=0, n=attn_w, tm=1024)
        kp_all, ks = proj(col0=attn_w, n=attn_w, tm=tm_kv, stack=kp_all, stacked=True)
        vp_all, vs = proj(col0=2 * attn_w, n=attn_w, tm=tm_kv, stack=vp_all, stacked=True)
        hr, rs = proj(col0=3 * attn_w, n=4 * attn_w, tm=1024)
        attn = _moba_prompt(qp, kp_all, vp_all, l, bias_diag, bias_adj, bias_far,
                            batch=bp, seq=seq, n_heads=n_heads)
        ret, s_new = _ret_prompt(hr, cos_p, sin_p, tabs, s0_prompt, batch=bp, seq=seq, n_heads=n_heads, col0=0)
        outs["rp"].append(s_new)
        x1_f, x1_b = _outproj_ln(attn, ret, w_out, l, xp_f, ln1_g3, ln1_b3, alpha=alpha, tm=512)
        gact, conv_new, ksum_l = _ffn_up_prompt(x1_b, w_gate, w_up, conv_w, conv_b3, conv0_prompt, l,
                                                page_table_flat, ck, batch=bp, seq=seq, tm=1024, tn=512,
                                                pages_per_block=ppb, n_heads=n_heads)
        ksum_l = ksum_l.reshape(bs, n_past_blocks, n_heads, hd)
        outs["cp"].append(conv_new)

        hs = jnp.concatenate([qs, ks, vs, rs], axis=1)[:bs]
        outs["ks"].append(hs[:, attn_w:2 * attn_w].reshape(bs, 1, n_heads, hd))
        outs["vs"].append(hs[:, 2 * attn_w:3 * attn_w].reshape(bs, 1, n_heads, hd))
        hs3 = hs.reshape(bs, 1, -1)
        picks = _sample_select(hs[:, :attn_w].reshape(bs, n_heads, hd), ksum_l)
        blocks = jnp.transpose(picks[..., 0], (0, 2, 1))
        sel_pos = blocks[..., None] * ppb + jnp.arange(ppb, dtype=jnp.int32)
        sel_pages = page_table[jnp.arange(bs)[:, None, None, None], sel_pos]
        near = (blocks == n_past_blocks - 1).astype(jnp.int32)
        attn_s = _sample_attn(hs3, cache_k, cache_v, l, sel_pages.reshape(-1), near.reshape(-1),
                              bias_near_s, bias_far_s, n_sel_pages=MOBA_TOPK * ppb)
        ret_s, s_new_s = _ret_sample(hs3, cos_s, sin_s, decay_s, state_ret[l], n_heads=n_heads, col0=3)
        outs["rs"].append(s_new_s)
        x1s_f, x1s_b = _outproj_ln(attn_s.reshape(bs, attn_w), ret_s.reshape(bs, attn_w), w_out, l, xs_f,
                                   ln1_g3, ln1_b3, alpha=alpha, tm=bs)
        gact_s, c0, c1 = _ffn_up_sample(x1s_b, w_gate, w_up, conv_w, conv_b3,
                                        state_conv[l, :, 0], state_conv[l, :, 1], l, tn=512)
        outs["cs"].append(jnp.stack([c0, c1], axis=1))

        gs_pad = jnp.pad(gact_s, ((0, SAMPLE_ROWS_PAD - bs), (0, 0)))
        f_p, f_s = _matmul(gact, gs_pad, w_down, l, col0=0, n=d, tm=512, tn=512)
        xp_f, xp_b = _ln_ple(f_p, x1_f, ln2_g3, ln2_b3, pp3, w_pg_b, w_ple, l, alpha=alpha, tm=512)
        xs_f, xs_b = _ln_ple(f_s[:bs], x1s_f, ln2_g3, ln2_b3, ps3, w_pg_b, w_ple, l, alpha=alpha, tm=bs)

    stack = lambda key: jnp.stack(outs[key])
    return (xp_f.reshape(bp, seq, d), xs_f.reshape(bs, 1, d),
            kp_all.reshape(depth, bp, seq, n_heads, hd), vp_all.reshape(depth, bp, seq, n_heads, hd),
            stack("ks"), stack("vs"), stack("rp"), stack("rs"), stack("cp"), stack("cs"))
```
